```python
import math
import jax, jax.numpy as jnp
from jax import lax
import numpy as np

D_MODEL = 1024
BATCH = 8
SEQ = 4096
DEPTH = 2

GDN_HEADS = 4
GDN_DK = 128
GDN_DV = 128
GDN_CONV = 4
RET_HEADS = 4
RET_DK = 64
RET_DV = 64
RWKV_HEADS = 4
RWKV_N = 64
RWKV_W_LORA = 32
RWKV_A_LORA = 32
RWKV_V_LORA = 32
RWKV_G_LORA = 64
D_FF = 4 * D_MODEL
CHUNK = 64
ROPE_BASE = 10000.0
NORM_EPS = 1e-6
L2_EPS = 1e-6
RET_GN_EPS = 1e-6
RWKV_GN_EPS = 64e-5

GDN_QK = GDN_HEADS * GDN_DK
GDN_V = GDN_HEADS * GDN_DV
GDN_CONV_DIM = 2 * GDN_QK + GDN_V
RET_QK = RET_HEADS * RET_DK
RET_V = RET_HEADS * RET_DV
RWKV_C = RWKV_HEADS * RWKV_N
D_MIX = GDN_V + RET_V + RWKV_C
GDN_IN = 2 * GDN_QK + 2 * GDN_V + 2 * GDN_HEADS
RET_IN = 2 * RET_QK + 2 * RET_V
RWKV_IN = 3 * RWKV_C + RWKV_W_LORA + RWKV_A_LORA + RWKV_G_LORA
D_IN = GDN_IN + RET_IN + RWKV_IN

kernel_name = 'hymba_style_gdn_retnet_rwkv7_adaln'


def _split(x, sizes):
    idx = [int(i) for i in np.cumsum(sizes)[:-1]]
    return jnp.split(x, idx, axis=-1)


def _rmsnorm(x, w):
    xf = x.astype(jnp.float32)
    y = xf * lax.rsqrt(jnp.mean(xf * xf, axis=-1, keepdims=True) + NORM_EPS)
    return (y * w.astype(jnp.float32)).astype(x.dtype)


def _l2norm(x):
    return x * lax.rsqrt(jnp.sum(x * x, axis=-1, keepdims=True) + L2_EPS)


def _head_layernorm(x, eps):
    mu = jnp.mean(x, axis=-1, keepdims=True)
    xc = x - mu
    return xc * lax.rsqrt(jnp.mean(xc * xc, axis=-1, keepdims=True) + eps)


def _causal_depthwise_conv(x, w):
    K, C = w.shape
    return lax.conv_general_dilated(
        x, w[:, None, :].astype(x.dtype), window_strides=(1,), padding=[(K - 1, 0)],
        dimension_numbers=('NWC', 'WIO', 'NWC'), feature_group_count=C)


def _rotary(x, cos, sin):
    half = x.shape[-1] // 2
    x1, x2 = x[..., :half], x[..., half:]
    return jnp.concatenate([x1 * cos - x2 * sin, x2 * cos + x1 * sin], axis=-1)


def _to_chunks(x):
    B, T, H, D = x.shape
    return x.reshape(B, T // CHUNK, CHUNK, H, D).transpose(0, 3, 1, 2, 4)


def _from_chunks(x):
    B, H, N, C, D = x.shape
    return x.transpose(0, 2, 3, 1, 4).reshape(B, N * C, H, D)


def _gated_delta_rule(q, k, v, g, beta):
    B, T, H, DK = q.shape
    DV = v.shape[-1]
    N = T // CHUNK
    q = _to_chunks(q * DK ** -0.5)
    k = _to_chunks(k)
    v = _to_chunks(v)
    g = g.reshape(B, N, CHUNK, H).transpose(0, 3, 1, 2)
    beta = beta.reshape(B, N, CHUNK, H).transpose(0, 3, 1, 2)[..., None]
    gc = jnp.cumsum(g, axis=-1)
    idx = jnp.arange(CHUNK)
    incl = idx[:, None] >= idx[None, :]
    strict = idx[:, None] > idx[None, :]
    diff = gc[..., :, None] - gc[..., None, :]
    decay = jnp.where(incl, jnp.exp(jnp.where(incl, diff, 0.0)), 0.0)
    k_beta = k * beta
    lower = jnp.einsum('bhncd,bhnsd->bhncs', k_beta, k) * jnp.where(strict, decay, 0.0)
    rhs = jnp.concatenate([v * beta, k_beta * jnp.exp(gc)[..., None]], axis=-1)
    sol = lax.linalg.triangular_solve(lower, rhs, left_side=True, lower=True, unit_diagonal=True)
    u, w = sol[..., :DV], sol[..., DV:]
    attn = jnp.einsum('bhncd,bhnsd->bhncs', q, k) * decay
    q_dec = q * jnp.exp(gc)[..., None]
    g_last = gc[..., -1]
    k_dec = k * jnp.exp(g_last[..., None] - gc)[..., None]
    xs = (jnp.moveaxis(u, 2, 0), jnp.moveaxis(w, 2, 0), jnp.moveaxis(q_dec, 2, 0),
          jnp.moveaxis(attn, 2, 0), jnp.moveaxis(k_dec, 2, 0), jnp.moveaxis(jnp.exp(g_last), 2, 0))

    def step(S, inp):
        u_n, w_n, q_n, a_n, k_n, d_n = inp
        v_new = u_n - jnp.einsum('bhcd,bhde->bhce', w_n, S)
        o = jnp.einsum('bhcd,bhde->bhce', q_n, S) + jnp.einsum('bhcs,bhse->bhce', a_n, v_new)
        S = S * d_n[..., None, None] + jnp.einsum('bhcd,bhce->bhde', k_n, v_new)
        return S, o

    S0 = jnp.zeros((B, H, DK, DV), jnp.float32)
    _, o = lax.scan(step, S0, xs)
    return _from_chunks(jnp.moveaxis(o, 0, 2))


def _retention(q, k, v, log_gamma):
    B, T, H, DK = q.shape
    DV = v.shape[-1]
    q = _to_chunks(q)
    k = _to_chunks(k)
    v = _to_chunks(v)
    idx = jnp.arange(CHUNK, dtype=jnp.float32)
    rel = idx[:, None] - idx[None, :]
    dmask = jnp.where(rel >= 0, jnp.exp(jnp.maximum(rel, 0.0) * log_gamma[:, None, None]), 0.0)
    scores = jnp.einsum('bhncd,bhnsd->bhncs', q, k) * dmask[:, None]
    inner = jnp.einsum('bhncs,bhnse->bhnce', scores, v)
    q_dec = q * jnp.exp((idx + 1.0)[None, :] * log_gamma[:, None])[:, None, :, None]
    k_dec = k * jnp.exp((CHUNK - 1.0 - idx)[None, :] * log_gamma[:, None])[:, None, :, None]
    chunk_decay = jnp.exp(CHUNK * log_gamma)[:, None, None]

    def step(R, inp):
        q_n, k_n, v_n = inp
        o = jnp.einsum('bhcd,bhde->bhce', q_n, R)
        R = R * chunk_decay + jnp.einsum('bhcd,bhce->bhde', k_n, v_n)
        return R, o

    R0 = jnp.zeros((B, H, DK, DV), jnp.float32)
    _, cross = lax.scan(step, R0, (jnp.moveaxis(q_dec, 2, 0), jnp.moveaxis(k_dec, 2, 0), jnp.moveaxis(v, 2, 0)))
    return _from_chunks(inner + jnp.moveaxis(cross, 0, 2))


def _rwkv7_recurrence(r, w, k, v, a, b):
    B, T, H, N = r.shape
    xs = (jnp.moveaxis(r, 1, 0), jnp.moveaxis(w, 1, 0), jnp.moveaxis(k, 1, 0),
          jnp.moveaxis(v, 1, 0), jnp.moveaxis(a, 1, 0), jnp.moveaxis(b, 1, 0))

    def step(S, inp):
        r_t, w_t, k_t, v_t, a_t, b_t = inp
        sa = jnp.einsum('bhvk,bhk->bhv', S, a_t)
        S = S * w_t[:, :, None, :] + sa[..., None] * b_t[:, :, None, :] + v_t[..., None] * k_t[:, :, None, :]
        return S, jnp.einsum('bhvk,bhk->bhv', S, r_t)

    S0 = jnp.zeros((B, H, N, N), jnp.float32)
    _, y = lax.scan(step, S0, xs)
    return jnp.moveaxis(y, 0, 1)


def _gdn_mixer(p, conv_w, a_log, dt_bias, norm_w):
    B, T, _ = p.shape
    qkv, z, b, a = _split(p, (GDN_CONV_DIM, GDN_V, GDN_HEADS, GDN_HEADS))
    qkv = jax.nn.silu(_causal_depthwise_conv(qkv, conv_w)).astype(jnp.float32)
    q, k, v = _split(qkv, (GDN_QK, GDN_QK, GDN_V))
    q = _l2norm(q.reshape(B, T, GDN_HEADS, GDN_DK))
    k = _l2norm(k.reshape(B, T, GDN_HEADS, GDN_DK))
    v = v.reshape(B, T, GDN_HEADS, GDN_DV)
    beta = jax.nn.sigmoid(b.astype(jnp.float32))
    g = -jnp.exp(a_log.astype(jnp.float32)) * jax.nn.softplus(a.astype(jnp.float32) + dt_bias.astype(jnp.float32))
    o = _gated_delta_rule(q, k, v, g, beta)
    o = _rmsnorm(o, norm_w) * jax.nn.silu(z.astype(jnp.float32).reshape(B, T, GDN_HEADS, GDN_DV))
    return o.reshape(B, T, GDN_V)


def _retention_mixer(p, rope_cos, rope_sin):
    B, T, _ = p.shape
    q, k, v, gate = _split(p.astype(jnp.float32), (RET_QK, RET_QK, RET_V, RET_V))
    q = _rotary(q.reshape(B, T, RET_HEADS, RET_DK), rope_cos, rope_sin)
    k = _rotary(k.reshape(B, T, RET_HEADS, RET_DK), rope_cos, rope_sin) * RET_DK ** -0.5
    log_gamma = jnp.log(1.0 - jnp.power(2.0, -5.0 - jnp.arange(RET_HEADS, dtype=jnp.float32)))
    o = _retention(q, k, v.reshape(B, T, RET_HEADS, RET_DV), log_gamma)
    o = _head_layernorm(o, RET_GN_EPS) * jax.nn.silu(gate.reshape(B, T, RET_HEADS, RET_DV))
    return o.reshape(B, T, RET_V)


def _rwkv7_mixer(p, v_first, mu, w0, w2, a0, a2, g2, k_k, k_a, r_k, ln_w, ln_b, v0, v1, v2):
    B, T, _ = p.shape
    p = p.astype(jnp.float32)
    p_prev = jnp.pad(p, ((0, 0), (1, 0), (0, 0)))[:, :-1]
    p = p + (p_prev - p) * mu
    r, k, v, wd, ad, gd = _split(p, (RWKV_C, RWKV_C, RWKV_C, RWKV_W_LORA, RWKV_A_LORA, RWKV_G_LORA))
    w_log = -jax.nn.softplus(-(w0 + jnp.tanh(wd) @ w2)) - 0.5
    decay = jnp.exp(-jnp.exp(w_log))
    a = jax.nn.sigmoid(a0 + ad @ a2)
    g = jax.nn.sigmoid(gd) @ g2
    if v_first is None:
        v_first = v
    else:
        v = v + (v_first - v) * jax.nn.sigmoid(v0 + (v @ v1) @ v2)
    hs = lambda t: t.reshape(B, T, RWKV_HEADS, RWKV_N)
    kk = _l2norm(hs(k * k_k))
    k = k * (1.0 + (a - 1.0) * k_a)
    r_h, k_h, v_h, a_h = hs(r), hs(k), hs(v), hs(a)
    y = _rwkv7_recurrence(r_h, hs(decay), k_h, v_h, -kk, kk * a_h)
    y = _head_layernorm(y, RWKV_GN_EPS).reshape(B, T, RWKV_C) * ln_w + ln_b
    bonus = jnp.sum(r_h * k_h * r_k, axis=-1, keepdims=True) * v_h
    out = (y + bonus.reshape(B, T, RWKV_C)) * g
    return out, v_first


def setup_inputs(seed: int = 0) -> dict:
    key = jax.random.key(seed)
    ks = jax.random.split(key, 32)
    f32 = jnp.float32
    L = DEPTH

    def nrm(k, shape, scale):
        return jax.random.normal(k, shape, f32) * scale

    def gain(k, shape):
        return 1.0 + 0.02 * jax.random.normal(k, shape, f32)

    x = nrm(ks[0], (BATCH, SEQ, D_MODEL), 1.0)
    c = nrm(ks[1], (BATCH, D_MODEL), 1.0)
    ada_w = nrm(ks[2], (L, D_MODEL, 6 * D_MODEL), 0.5 * D_MODEL ** -0.5)
    ada_b = nrm(ks[3], (L, 6 * D_MODEL), 0.01)
    norm1_w = gain(ks[4], (L, D_MODEL))
    norm2_w = gain(ks[5], (L, D_MODEL))
    w_in = nrm(ks[6], (L, D_MODEL, D_IN), D_MODEL ** -0.5)
    gdn_conv_w = nrm(ks[7], (L, GDN_CONV, GDN_CONV_DIM), GDN_CONV ** -0.5)
    gdn_a_log = jnp.log(jax.random.uniform(ks[8], (L, GDN_HEADS), f32, minval=1.0, maxval=16.0))
    dt = jnp.exp(jax.random.uniform(ks[9], (L, GDN_HEADS), f32, minval=math.log(1e-3), maxval=math.log(1e-1)))
    gdn_dt_bias = dt + jnp.log(-jnp.expm1(-dt))
    gdn_norm_w = gain(ks[10], (L, GDN_DV))
    rwkv_mu = jax.random.uniform(ks[11], (L, RWKV_IN), f32)
    rwkv_w0 = jax.random.uniform(ks[12], (L, RWKV_C), f32, minval=-6.5, maxval=-1.5)
    rwkv_w2 = nrm(ks[13], (L, RWKV_W_LORA, RWKV_C), 0.1 * RWKV_W_LORA ** -0.5)
    rwkv_a0 = nrm(ks[14], (L, RWKV_C), 0.1)
    rwkv_a2 = nrm(ks[15], (L, RWKV_A_LORA, RWKV_C), 0.5 * RWKV_A_LORA ** -0.5)
    rwkv_g2 = nrm(ks[16], (L, RWKV_G_LORA, RWKV_C), RWKV_G_LORA ** -0.5)
    rwkv_k_k = 0.85 + nrm(ks[17], (L, RWKV_C), 0.05)
    rwkv_k_a = 1.0 + nrm(ks[18], (L, RWKV_C), 0.05)
    rwkv_r_k = nrm(ks[19], (L, RWKV_HEADS, RWKV_N), 0.1)
    rwkv_ln_w = gain(ks[20], (L, RWKV_C))
    rwkv_ln_b = nrm(ks[21], (L, RWKV_C), 0.01)
    rwkv_v0 = 1.0 + nrm(ks[22], (L - 1, RWKV_C), 0.05)
    rwkv_v1 = nrm(ks[23], (L - 1, RWKV_C, RWKV_V_LORA), 0.5 * RWKV_C ** -0.5)
    rwkv_v2 = nrm(ks[24], (L - 1, RWKV_V_LORA, RWKV_C), 0.5 * RWKV_V_LORA ** -0.5)
    w_out = nrm(ks[25], (L, D_MIX, D_MODEL), D_MIX ** -0.5)
    w_up = nrm(ks[26], (L, D_MODEL, D_FF), D_MODEL ** -0.5)
    w_down = nrm(ks[27], (L, D_FF, D_MODEL), D_FF ** -0.5)
    final_norm_w = gain(ks[28], (D_MODEL,))
    return {'x': x, 'c': c, 'ada_w': ada_w, 'ada_b': ada_b, 'norm1_w': norm1_w, 'norm2_w': norm2_w,
            'w_in': w_in, 'gdn_conv_w': gdn_conv_w, 'gdn_a_log': gdn_a_log, 'gdn_dt_bias': gdn_dt_bias,
            'gdn_norm_w': gdn_norm_w, 'rwkv_mu': rwkv_mu, 'rwkv_w0': rwkv_w0, 'rwkv_w2': rwkv_w2,
            'rwkv_a0': rwkv_a0, 'rwkv_a2': rwkv_a2, 'rwkv_g2': rwkv_g2, 'rwkv_k_k': rwkv_k_k,
            'rwkv_k_a': rwkv_k_a, 'rwkv_r_k': rwkv_r_k, 'rwkv_ln_w': rwkv_ln_w, 'rwkv_ln_b': rwkv_ln_b,
            'rwkv_v0': rwkv_v0, 'rwkv_v1': rwkv_v1, 'rwkv_v2': rwkv_v2, 'w_out': w_out,
            'w_up': w_up, 'w_down': w_down, 'final_norm_w': final_norm_w}


def reference(x, c, ada_w, ada_b, norm1_w, norm2_w, w_in, gdn_conv_w, gdn_a_log, gdn_dt_bias,
              gdn_norm_w, rwkv_mu, rwkv_w0, rwkv_w2, rwkv_a0, rwkv_a2, rwkv_g2, rwkv_k_k,
              rwkv_k_a, rwkv_r_k, rwkv_ln_w, rwkv_ln_b, rwkv_v0, rwkv_v1, rwkv_v2, w_out,
              w_up, w_down, final_norm_w):
    T = x.shape[1]
    pos = jnp.arange(T, dtype=jnp.float32)
    inv_freq = ROPE_BASE ** (-jnp.arange(RET_DK // 2, dtype=jnp.float32) / (RET_DK // 2))
    ang = pos[:, None] * inv_freq[None, :]
    rope_cos = jnp.cos(ang)[:, None, :]
    rope_sin = jnp.sin(ang)[:, None, :]
    cond = jax.nn.silu(c)
    v_first = None
    for l in range(DEPTH):
        mod = cond @ ada_w[l] + ada_b[l]
        sh1, sc1, g1, sh2, sc2, g2 = jnp.split(mod[:, None, :], 6, axis=-1)
        h = _rmsnorm(x, norm1_w[l]) * (1.0 + sc1) + sh1
        p = h @ w_in[l]
        p_gdn, p_ret, p_rwkv = _split(p, (GDN_IN, RET_IN, RWKV_IN))
        o_a = _gdn_mixer(p_gdn, gdn_conv_w[l], gdn_a_log[l], gdn_dt_bias[l], gdn_norm_w[l])
        o_b = _retention_mixer(p_ret, rope_cos, rope_sin)
        if l == 0:
            o_c, v_first = _rwkv7_mixer(p_rwkv, None, rwkv_mu[l], rwkv_w0[l], rwkv_w2[l], rwkv_a0[l],
                                        rwkv_a2[l], rwkv_g2[l], rwkv_k_k[l], rwkv_k_a[l], rwkv_r_k[l],
                                        rwkv_ln_w[l], rwkv_ln_b[l], None, None, None)
        else:
            o_c, v_first = _rwkv7_mixer(p_rwkv, v_first, rwkv_mu[l], rwkv_w0[l], rwkv_w2[l], rwkv_a0[l],
                                        rwkv_a2[l], rwkv_g2[l], rwkv_k_k[l], rwkv_k_a[l], rwkv_r_k[l],
                                        rwkv_ln_w[l], rwkv_ln_b[l], rwkv_v0[l - 1], rwkv_v1[l - 1],
                                        rwkv_v2[l - 1])
        mix = jnp.concatenate([o_a, o_b, o_c], axis=-1).astype(x.dtype)
        x = x + g1 * (mix @ w_out[l])
        h = _rmsnorm(x, norm2_w[l]) * (1.0 + sc2) + sh2
        x = x + g2 * (jnp.square(jax.nn.relu(h @ w_up[l])) @ w_down[l])
    return _rmsnorm(x, final_norm_w)
```

```python
import functools
import math

import jax
import jax.numpy as jnp
from jax import lax
from jax.experimental import pallas as pl
from jax.experimental.pallas import tpu as pltpu

f32 = jnp.float32
bf16 = jnp.bfloat16

D_MODEL = 1024
D_FF = 4 * D_MODEL
HEADS = 4
GDN_D = 128
GDN_W = HEADS * GDN_D
GDN_CONV = 4
HD = 64
HW = HEADS * HD
CHUNK = 64
ROPE_BASE = 10000.0
NORM_EPS = 1e-6
L2_EPS = 1e-6
RET_GN_EPS = 1e-6
RWKV_GN_EPS = 64e-5
RWKV_LORA_W = 128

P_GDN = 0
P_RET = 2048
P_RWKV = 3072
P_BA = 3968
P_COLS = 4096

VMEM_LIMIT = 48 * 1024 * 1024


def _iota(shape, dim):
    return lax.broadcasted_iota(jnp.int32, shape, dim)


def _sigmoid(x):
    return jax.nn.sigmoid(x)


def _silu(x):
    return x * jax.nn.sigmoid(x)


def _softplus(x):
    return jnp.maximum(x, 0.0) + jnp.log1p(jnp.exp(-jnp.abs(x)))


_NN = (((1,), (0,)), ((), ()))
_NT = (((1,), (1,)), ((), ()))
_TN = (((0,), (0,)), ((), ()))


def _dg(a, b, dims):
    return lax.dot_general(a, b, dims, preferred_element_type=f32)


def _mm(a, b, dims=_NN):
    return _dg(a.astype(bf16), b.astype(bf16), dims)


def _split3(x):
    hi = x.astype(bf16)
    r1 = x - hi.astype(f32)
    mid = r1.astype(bf16)
    lo = (r1 - mid.astype(f32)).astype(bf16)
    return hi, mid, lo


def _split2(x):
    hi = x.astype(bf16)
    lo = (x - hi.astype(f32)).astype(bf16)
    return hi, lo


def _mm_lexact(a_bf, b, dims=_NN):
    b0, b1, b2 = _split3(b)
    return _dg(a_bf, b0, dims) + _dg(a_bf, b1, dims) + _dg(a_bf, b2, dims)


def _mm_rexact(a, b_bf, dims=_NN):
    a0, a1, a2 = _split3(a)
    return _dg(a0, b_bf, dims) + _dg(a1, b_bf, dims) + _dg(a2, b_bf, dims)


def _mm3(a, b, dims=_NN):
    a0, a1 = _split2(a)
    b0, b1 = _split2(b)
    return _dg(a0, b0, dims) + (_dg(a0, b1, dims) + _dg(a1, b0, dims))


def _tri_inv(low, eye):
    t = eye + low
    p = low
    for _ in range(5):
        p = _mm3(p, p)
        t = t + _mm3(t, p)
    return t


def _expand_heads(x, head_masks):
    return jnp.concatenate([jnp.where(m, x, 0.0) for m in head_masks], axis=0)


def _collapse_heads(x, c):
    return x[0:c] + x[c:2 * c] + x[2 * c:3 * c] + x[3 * c:4 * c]


def _head_masks():
    lane = _iota((1, HW), 1)
    return [(lane // HD) == h for h in range(HEADS)]


def _block_mask():
    return (_iota((HW, HW), 0) // HD) == (_iota((HW, HW), 1) // HD)


def _mod_body(c_ref, w_ref, b_ref, o_ref):
    cond = _silu(c_ref[...])
    o_ref[0] = _mm3(cond, w_ref[0]) + b_ref[0]


def _modulation(c, ada_w, ada_b):
    depth, _, n = ada_w.shape
    bsz = c.shape[0]
    nb = 512
    return pl.pallas_call(
        _mod_body,
        out_shape=jax.ShapeDtypeStruct((depth, bsz, n), f32),
        grid=(depth, n // nb),
        in_specs=[pl.BlockSpec((bsz, D_MODEL), lambda l, j: (0, 0)),
                  pl.BlockSpec((1, D_MODEL, nb), lambda l, j: (l, 0, j)),
                  pl.BlockSpec((1, 1, nb), lambda l, j: (l, 0, j))],
        out_specs=pl.BlockSpec((1, bsz, nb), lambda l, j: (l, 0, j)),
        compiler_params=pltpu.CompilerParams(dimension_semantics=("parallel", "parallel")),
        name="adaln_mod",
    )(c, ada_w, ada_b.reshape(depth, 1, n))


def _rope_body(cos_ref, sin_ref, *, tt):
    shape = (tt, HW)
    pos = (pl.program_id(0) * tt + _iota(shape, 0)).astype(f32)
    lane = _iota(shape, 1)
    half = HD // 2
    inv_freq = jnp.exp((lane % half).astype(f32) * (-math.log(ROPE_BASE) / half))
    ang = pos * inv_freq
    cos_ref[...] = jnp.cos(ang)
    s = jnp.sin(ang)
    sin_ref[...] = jnp.where((lane % HD) < half, -s, s)


def _rope_tables(t):
    tt = min(t, 512)
    return pl.pallas_call(
        functools.partial(_rope_body, tt=tt),
        out_shape=(jax.ShapeDtypeStruct((t, HW), f32), jax.ShapeDtypeStruct((t, HW), f32)),
        grid=(t // tt,),
        out_specs=(pl.BlockSpec((tt, HW), lambda i: (i, 0)), pl.BlockSpec((tt, HW), lambda i: (i, 0))),
        compiler_params=pltpu.CompilerParams(dimension_semantics=("parallel",)),
        name="rope_tables",
    )()


def _proj_body(x_ref, mod_ref, nw_ref, w_ref, p_ref):
    x = x_ref[0]
    m = mod_ref[0]
    y = x * lax.rsqrt(jnp.mean(x * x, axis=-1, keepdims=True) + NORM_EPS) * nw_ref[...]
    h = y * (1.0 + m[1:2]) + m[0:1]
    p_ref[0] = jnp.dot(h.astype(bf16), w_ref[...], preferred_element_type=f32)


def _projection(x, mod, norm_w, w_in_r, tm):
    bsz, t, _ = x.shape
    return pl.pallas_call(
        _proj_body,
        out_shape=jax.ShapeDtypeStruct((bsz, t, P_COLS), f32),
        grid=(bsz, t // tm),
        in_specs=[pl.BlockSpec((1, tm, D_MODEL), lambda b, i: (b, i, 0)),
                  pl.BlockSpec((1, 6, D_MODEL), lambda b, i: (b, 0, 0)),
                  pl.BlockSpec((1, D_MODEL), lambda b, i: (0, 0)),
                  pl.BlockSpec((D_MODEL, P_COLS), lambda b, i: (0, 0), pipeline_mode=pl.Buffered(1))],
        out_specs=pl.BlockSpec((1, tm, P_COLS), lambda b, i: (b, i, 0)),
        compiler_params=pltpu.CompilerParams(dimension_semantics=("parallel", "parallel"),
                                             vmem_limit_bytes=VMEM_LIMIT),
        name="in_proj",
    )(x, mod, norm_w.reshape(1, D_MODEL), w_in_r)


def _gdn_body(qkvz_ref, ba_ref, cw_ref, gp_ref, nw_ref, o_ref,
              s_ref, carry_ref, q_s, k_s, v_s, gc_s, beta_s, *, tt):
    c = CHUNK

    @pl.when(pl.program_id(1) == 0)
    def _():
        s_ref[...] = jnp.zeros_like(s_ref)
        carry_ref[...] = jnp.zeros_like(carry_ref)

    for gi, dst in enumerate((q_s, k_s, v_s)):
        cols = slice(gi * GDN_W, (gi + 1) * GDN_W)
        raw = qkvz_ref[0, :, cols]
        xw = jnp.concatenate([carry_ref[:, cols], raw], axis=0)
        cw = cw_ref[:, cols]
        y = (cw[3:4] * raw + cw[2:3] * xw[7:7 + tt] + cw[1:2] * xw[6:6 + tt] + cw[0:1] * xw[5:5 + tt])
        y = _silu(y)
        carry_ref[:, cols] = raw[tt - 8:tt]
        if gi < 2:
            for h in range(HEADS):
                seg = y[:, h * GDN_D:(h + 1) * GDN_D]
                dst[:, h * GDN_D:(h + 1) * GDN_D] = seg * lax.rsqrt(
                    jnp.sum(seg * seg, axis=-1, keepdims=True) + L2_EPS)
        else:
            dst[...] = y

    ba = ba_ref[0]
    beta_s[...] = _sigmoid(ba)
    gp = gp_ref[...]
    g = -jnp.exp(gp[0:1]) * _softplus(ba + gp[1:2])
    ii, jj = _iota((tt, tt), 0), _iota((tt, tt), 1)
    cum_mat = jnp.where(((ii // c) == (jj // c)) & (jj <= ii), 1.0, 0.0).astype(bf16)
    gc_s[...] = _mm_lexact(cum_mat, g)

    ci, cj = _iota((c, c), 0), _iota((c, c), 1)
    incl = ci >= cj
    strict = ci > cj
    eye = jnp.where(ci == cj, 1.0, 0.0)
    sel = jnp.where(_iota((8, 128), 1) == _iota((8, 128), 0) + HEADS, 1.0, 0.0).astype(bf16)
    nw = nw_ref[...]
    scale = GDN_D ** -0.5

    def chunk(ic, carry):
        r = pl.multiple_of(ic * c, c)
        rows = pl.ds(r, c)
        gcb = gc_s[rows, :]
        gcrows = _mm_lexact(sel, gcb, _NT)
        betab = beta_s[rows, :]
        for h in range(HEADS):
            hs = slice(h * GDN_D, (h + 1) * GDN_D)
            q = q_s[rows, hs] * scale
            k = k_s[rows, hs]
            v = v_s[rows, hs]
            beta = betab[:, h:h + 1]
            gc = gcb[:, HEADS + h:HEADS + h + 1]
            gcr = gcrows[h:h + 1, :]
            g_last = gcb[c - 1:c, HEADS + h:HEADS + h + 1]
            dec = jnp.where(incl, jnp.exp(jnp.where(incl, gc - gcr, 0.0)), 0.0)
            kb = k * beta
            lower = _mm(kb, k, _NT) * jnp.where(strict, dec, 0.0)
            tmat = _tri_inv(-lower, eye)
            eg = jnp.exp(gc)
            sol = _mm3(tmat, jnp.concatenate([v * beta, kb * eg], axis=1))
            u, w = sol[:, :GDN_D], sol[:, GDN_D:]
            attn = _mm(q, k, _NT) * dec
            q_dec = q * eg
            k_dec = k * jnp.exp(g_last - gc)
            s = s_ref[h]
            v_new = u - _mm(w, s)
            o = _mm(q_dec, s) + _mm(attn, v_new)
            s_ref[h] = s * jnp.exp(g_last) + _mm(k_dec, v_new, _TN)
            o = o * lax.rsqrt(jnp.mean(o * o, axis=-1, keepdims=True) + NORM_EPS) * nw
            z = qkvz_ref[0, rows, 3 * GDN_W + h * GDN_D:3 * GDN_W + (h + 1) * GDN_D]
            o_ref[0, rows, hs] = o * _silu(z)
        return carry

    lax.fori_loop(0, tt // c, chunk, 0)


def _gdn(p, conv_w, a_log, dt_bias, norm_w, tt):
    bsz, t, _ = p.shape
    gp = jnp.zeros((8, 128), f32)
    gp = gp.at[0, HEADS:2 * HEADS].set(a_log).at[1, HEADS:2 * HEADS].set(dt_bias)
    return pl.pallas_call(
        functools.partial(_gdn_body, tt=tt),
        out_shape=jax.ShapeDtypeStruct((bsz, t, GDN_W), f32),
        grid=(bsz, t // tt),
        in_specs=[pl.BlockSpec((1, tt, 4 * GDN_W), lambda b, i: (b, i, P_GDN // (4 * GDN_W))),
                  pl.BlockSpec((1, tt, 128), lambda b, i: (b, i, P_BA // 128)),
                  pl.BlockSpec((GDN_CONV, 3 * GDN_W), lambda b, i: (0, 0)),
                  pl.BlockSpec((8, 128), lambda b, i: (0, 0)),
                  pl.BlockSpec((1, GDN_D), lambda b, i: (0, 0))],
        out_specs=pl.BlockSpec((1, tt, GDN_W), lambda b, i: (b, i, 0)),
        scratch_shapes=[pltpu.VMEM((HEADS, GDN_D, GDN_D), f32),
                        pltpu.VMEM((8, 3 * GDN_W), f32),
                        pltpu.VMEM((tt, GDN_W), f32),
                        pltpu.VMEM((tt, GDN_W), f32),
                        pltpu.VMEM((tt, GDN_W), f32),
                        pltpu.VMEM((tt, 128), f32),
                        pltpu.VMEM((tt, 128), f32)],
        compiler_params=pltpu.CompilerParams(dimension_semantics=("parallel", "arbitrary"),
                                             vmem_limit_bytes=VMEM_LIMIT),
        name="gdn_mixer",
    )(p, p, conv_w, gp, norm_w.reshape(1, GDN_D))


def _seg_stat(x, blk_bf):
    return _mm_rexact(x, blk_bf)


def _ret_body(p_ref, cos_ref, sin_ref, o_ref, r_ref, q_s, k_s, *, tt):
    c = CHUNK

    @pl.when(pl.program_id(1) == 0)
    def _():
        r_ref[...] = jnp.zeros_like(r_ref)

    lane_t = _iota((tt, HW), 1)
    first_half = (lane_t % HD) < (HD // 2)
    cosf = cos_ref[...]
    sins = sin_ref[...]

    def rotary(x):
        swapped = jnp.where(first_half, pltpu.roll(x, HW - HD // 2, 1), pltpu.roll(x, HD // 2, 1))
        return x * cosf + swapped * sins

    q_s[...] = rotary(p_ref[0, :, 0:HW])
    k_s[...] = rotary(p_ref[0, :, HW:2 * HW]) * (HD ** -0.5)

    hm = _head_masks()
    blk = _block_mask()
    blk_bf = jnp.where(blk, 1.0, 0.0).astype(bf16)
    lane = _iota((1, HW), 1)
    lg = jnp.zeros((1, HW), f32)
    for h in range(HEADS):
        lg = jnp.where((lane // HD) == h, math.log(1.0 - 2.0 ** (-5.0 - h)), lg)
    pos = _iota((c, HW), 0).astype(f32)
    q_decay = jnp.exp((pos + 1.0) * lg)
    k_decay = jnp.exp((c - 1.0 - pos) * lg)
    chunk_decay = jnp.exp(float(c) * lg)
    rel = (_iota((c, HW), 0) - (_iota((c, HW), 1) % c)).astype(f32)
    dmask = jnp.where(rel >= 0, jnp.exp(jnp.maximum(rel, 0.0) * lg), 0.0)

    def chunk(ic, carry):
        rows = pl.ds(pl.multiple_of(ic * c, c), c)
        q = q_s[rows, :]
        k = k_s[rows, :]
        v = p_ref[0, rows, 2 * HW:3 * HW]
        gate = p_ref[0, rows, 3 * HW:4 * HW]
        ke = _expand_heads(k, hm)
        ve = _expand_heads(v, hm)
        scores = _mm(q, ke, _NT) * dmask
        state = r_ref[...]
        o = _mm(scores, ve) + _mm(q * q_decay, state)
        r_ref[...] = state * chunk_decay + jnp.where(blk, _mm(k * k_decay, v, _TN), 0.0)
        mu = _seg_stat(o, blk_bf) * (1.0 / HD)
        oc = o - mu
        var = _seg_stat(oc * oc, blk_bf) * (1.0 / HD)
        o_ref[0, rows, :] = oc * lax.rsqrt(var + RET_GN_EPS) * _silu(gate)
        return carry

    lax.fori_loop(0, tt // c, chunk, 0)


def _retention(p, cos_t, sin_t, tt):
    bsz, t, _ = p.shape
    return pl.pallas_call(
        functools.partial(_ret_body, tt=tt),
        out_shape=jax.ShapeDtypeStruct((bsz, t, HW), f32),
        grid=(bsz, t // tt),
        in_specs=[pl.BlockSpec((1, tt, 4 * HW), lambda b, i: (b, i, P_RET // (4 * HW))),
                  pl.BlockSpec((tt, HW), lambda b, i: (i, 0)),
                  pl.BlockSpec((tt, HW), lambda b, i: (i, 0))],
        out_specs=pl.BlockSpec((1, tt, HW), lambda b, i: (b, i, 0)),
        scratch_shapes=[pltpu.VMEM((HW, HW), f32),
                        pltpu.VMEM((tt, HW), f32),
                        pltpu.VMEM((tt, HW), f32)],
        compiler_params=pltpu.CompilerParams(dimension_semantics=("parallel", "arbitrary"),
                                             vmem_limit_bytes=VMEM_LIMIT),
        name="retention_mixer",
    )(p, cos_t, sin_t)


_RW_W0, _RW_A0, _RW_KK, _RW_KA, _RW_RK, _RW_LNW, _RW_LNB, _RW_V0 = range(8)


def _rwkv_body(*refs, tt, first):
    if first:
        (p_ref, mu_ref, par_ref, w2_ref, a2_ref, g2_ref, o_ref, vf_out_ref,
         s_ref, carry_ref, r_s, lw_s, k_s, v_s, a_s, b_s, g_s, bon_s) = refs
    else:
        (p_ref, vf_ref, mu_ref, par_ref, w2_ref, a2_ref, g2_ref, v1_ref, v2_ref, o_ref,
         s_ref, carry_ref, r_s, lw_s, k_s, v_s, a_s, b_s, g_s, bon_s) = refs
    c = CHUNK

    @pl.when(pl.program_id(1) == 0)
    def _():
        s_ref[...] = jnp.zeros_like(s_ref)
        carry_ref[...] = jnp.zeros_like(carry_ref)

    blk = _block_mask()
    blk_bf = jnp.where(blk, 1.0, 0.0).astype(bf16)
    par = par_ref[...]
    row = lambda i: par[i:i + 1]

    raw = p_ref[0]
    xw = jnp.concatenate([carry_ref[...], raw], axis=0)
    carry_ref[...] = raw[tt - 8:tt]
    pm = raw + (xw[7:7 + tt] - raw) * mu_ref[...]
    r = pm[:, 0:HW]
    k = pm[:, HW:2 * HW]
    v = pm[:, 2 * HW:3 * HW]
    lora = pm[:, 3 * HW:3 * HW + RWKV_LORA_W]
    w_log = -_softplus(-(row(_RW_W0) + _mm(jnp.tanh(lora), w2_ref[...]))) - 0.5
    lw_s[...] = -jnp.exp(w_log)
    a = _sigmoid(row(_RW_A0) + _mm(lora, a2_ref[...]))
    g_s[...] = _mm(_sigmoid(lora), g2_ref[...])
    if first:
        vf_out_ref[0] = v
    else:
        mix = _sigmoid(row(_RW_V0) + _mm(_mm(v, v1_ref[...]), v2_ref[...]))
        v = v + (vf_ref[0] - v) * mix
    kk = k * row(_RW_KK)
    kk = kk * lax.rsqrt(_seg_stat(kk * kk, blk_bf) + L2_EPS)
    k = k * (1.0 + (a - 1.0) * row(_RW_KA))
    r_s[...] = r
    k_s[...] = k
    v_s[...] = v
    a_s[...] = -kk
    b_s[...] = kk * a
    bon_s[...] = _seg_stat(r * k * row(_RW_RK), blk_bf) * v

    hm = _head_masks()
    ci, cj = _iota((c, c), 0), _iota((c, c), 1)
    cum_mat = jnp.where(cj <= ci, 1.0, 0.0).astype(bf16)
    bi, bj = _iota((HW, HW), 0), _iota((HW, HW), 1)
    strict_blk = bi > bj
    eye_blk = jnp.where(bi == bj, 1.0, 0.0)
    incl_side = _iota((c, HW), 0) >= (_iota((c, HW), 1) % c)
    ln_w, ln_b = row(_RW_LNW), row(_RW_LNB)

    def chunk(ic, carry):
        rows = pl.ds(pl.multiple_of(ic * c, c), c)
        r = r_s[rows, :]
        lw = lw_s[rows, :]
        k = k_s[rows, :]
        v = v_s[rows, :]
        a_ = a_s[rows, :]
        b_ = b_s[rows, :]
        gcum = _mm_lexact(cum_mat, lw)
        g_last = gcum[c - 1:c, :]
        e_neg = jnp.exp(-gcum)
        rt = r * jnp.exp(gcum)
        at = a_ * jnp.exp(gcum - lw)
        kt = k * e_neg
        bt = b_ * e_neg
        e_rem = jnp.exp(g_last - gcum)
        ae, ke, be, ve = (_expand_heads(x, hm) for x in (at, kt, bt, v))
        a_ab = jnp.where(strict_blk, _mm(ae, be, _NT), 0.0)
        a_ak = jnp.where(strict_blk, _mm(ae, ke, _NT), 0.0)
        tmat = _tri_inv(a_ab, eye_blk)
        wu = _collapse_heads(_mm3(tmat, jnp.concatenate([ae, _mm(a_ak, ve)], axis=1)), c)
        wp, up = wu[:, :HW], wu[:, HW:]
        a_rk = jnp.where(incl_side, _mm(rt, ke, _NT), 0.0)
        a_rb = jnp.where(incl_side, _mm(rt, be, _NT), 0.0)
        state = s_ref[...]
        u = up + _mm(wp, state, _NT)
        y = _mm(rt, state, _NT) + _mm(a_rk, ve) + _mm(a_rb, _expand_heads(u, hm))
        upd = _mm(jnp.concatenate([v, u], axis=0), jnp.concatenate([k * e_rem, b_ * e_rem], axis=0), _TN)
        s_ref[...] = state * jnp.exp(g_last) + jnp.where(blk, upd, 0.0)
        mu = _seg_stat(y, blk_bf) * (1.0 / HD)
        yc = y - mu
        var = _seg_stat(yc * yc, blk_bf) * (1.0 / HD)
        yn = yc * lax.rsqrt(var + RWKV_GN_EPS) * ln_w + ln_b
        o_ref[0, rows, :] = (yn + bon_s[rows, :]) * g_s[rows, :]
        return carry

    lax.fori_loop(0, tt // c, chunk, 0)


def _pad_rows(w, start, total):
    return jnp.zeros((total, w.shape[1]), f32).at[start:start + w.shape[0]].set(w)


def _rwkv(p, v_first, mu, w0, w2, a0, a2, g2, k_k, k_a, r_k, ln_w, ln_b, v0, v1, v2, tt):
    bsz, t, _ = p.shape
    first = v_first is None
    mu_pad = jnp.zeros((1, 4 * HW), f32).at[0, :mu.shape[0]].set(mu)
    par = jnp.stack([w0, a0, k_k, k_a, r_k.reshape(HW), ln_w, ln_b,
                     jnp.zeros((HW,), f32) if first else v0], axis=0)
    w2p = _pad_rows(w2, 0, RWKV_LORA_W).astype(bf16)
    a2p = _pad_rows(a2, 32, RWKV_LORA_W).astype(bf16)
    g2p = _pad_rows(g2, 64, RWKV_LORA_W).astype(bf16)
    full = lambda shape: pl.BlockSpec(shape, lambda b, i: (0,) * len(shape))
    tile = pl.BlockSpec((1, tt, HW), lambda b, i: (b, i, 0))
    p_spec = pl.BlockSpec((1, tt, 4 * HW), lambda b, i: (b, i, P_RWKV // (4 * HW)))
    common = [full((1, 4 * HW)), full((8, HW)), full((RWKV_LORA_W, HW)), full((RWKV_LORA_W, HW)),
              full((RWKV_LORA_W, HW))]
    if first:
        in_specs = [p_spec] + common
        args = (p, mu_pad, par, w2p, a2p, g2p)
        out_shape = (jax.ShapeDtypeStruct((bsz, t, HW), f32), jax.ShapeDtypeStruct((bsz, t, HW), f32))
        out_specs = (tile, tile)
    else:
        v1p = jnp.zeros((HW, 128), f32).at[:, :v1.shape[1]].set(v1).astype(bf16)
        v2p = _pad_rows(v2, 0, 128).astype(bf16)
        in_specs = [p_spec, tile] + common + [full((HW, 128)), full((128, HW))]
        args = (p, v_first, mu_pad, par, w2p, a2p, g2p, v1p, v2p)
        out_shape = jax.ShapeDtypeStruct((bsz, t, HW), f32)
        out_specs = tile
    res = pl.pallas_call(
        functools.partial(_rwkv_body, tt=tt, first=first),
        out_shape=out_shape,
        grid=(bsz, t // tt),
        in_specs=in_specs,
        out_specs=out_specs,
        scratch_shapes=[pltpu.VMEM((HW, HW), f32),
                        pltpu.VMEM((8, 4 * HW), f32)]
                       + [pltpu.VMEM((tt, HW), f32) for _ in range(8)],
        compiler_params=pltpu.CompilerParams(dimension_semantics=("parallel", "arbitrary"),
                                             vmem_limit_bytes=VMEM_LIMIT),
        name="rwkv7_mixer_first" if first else "rwkv7_mixer",
    )(*args)
    return res if first else (res, v_first)


def _mlp_body(x_ref, oa_ref, ob_ref, oc_ref, mod_ref, nw_ref, wo_ref, wu_ref, wd_ref, fw_ref, o_ref, *, last):
    m = mod_ref[0]
    mix = (jnp.dot(oa_ref[0].astype(bf16), wo_ref[0:GDN_W, :], preferred_element_type=f32)
           + jnp.dot(ob_ref[0].astype(bf16), wo_ref[GDN_W:GDN_W + HW, :], preferred_element_type=f32)
           + jnp.dot(oc_ref[0].astype(bf16), wo_ref[GDN_W + HW:, :], preferred_element_type=f32))
    x = x_ref[0] + m[2:3] * mix
    y = x * lax.rsqrt(jnp.mean(x * x, axis=-1, keepdims=True) + NORM_EPS) * nw_ref[...]
    h = (y * (1.0 + m[4:5]) + m[3:4]).astype(bf16)
    acc = jnp.zeros(x.shape, f32)
    nb = 1024
    for j in range(D_FF // nb):
        hid = jnp.dot(h, wu_ref[:, j * nb:(j + 1) * nb], preferred_element_type=f32)
        hid = jnp.square(jnp.maximum(hid, 0.0))
        acc = acc + jnp.dot(hid.astype(bf16), wd_ref[j * nb:(j + 1) * nb, :], preferred_element_type=f32)
    x = x + m[5:6] * acc
    if last:
        x = x * lax.rsqrt(jnp.mean(x * x, axis=-1, keepdims=True) + NORM_EPS) * fw_ref[...]
    o_ref[0] = x


def _out_mlp(x, oa, ob, oc, mod, norm_w, w_out, w_up, w_down, final_w, last, tm):
    bsz, t, _ = x.shape
    tile = lambda w: pl.BlockSpec((1, tm, w), lambda b, i: (b, i, 0))
    const = lambda shape: pl.BlockSpec(shape, lambda b, i: (0, 0), pipeline_mode=pl.Buffered(1))
    return pl.pallas_call(
        functools.partial(_mlp_body, last=last),
        out_shape=jax.ShapeDtypeStruct((bsz, t, D_MODEL), f32),
        grid=(bsz, t // tm),
        in_specs=[tile(D_MODEL), tile(GDN_W), tile(HW), tile(HW),
                  pl.BlockSpec((1, 6, D_MODEL), lambda b, i: (b, 0, 0)),
                  pl.BlockSpec((1, D_MODEL), lambda b, i: (0, 0)),
                  const((D_MODEL, D_MODEL)), const((D_MODEL, D_FF)), const((D_FF, D_MODEL)),
                  pl.BlockSpec((1, D_MODEL), lambda b, i: (0, 0))],
        out_specs=tile(D_MODEL),
        compiler_params=pltpu.CompilerParams(dimension_semantics=("parallel", "parallel"),
                                             vmem_limit_bytes=VMEM_LIMIT),
        name="out_mlp_last" if last else "out_mlp",
    )(x, oa, ob, oc, mod, norm_w.reshape(1, D_MODEL), w_out, w_up, w_down, final_w.reshape(1, D_MODEL))


def _reorder_w_in(w):
    gdn_main = 4 * GDN_W
    ba = 2 * HEADS
    pad = jnp.zeros((w.shape[0], P_COLS - w.shape[1]), w.dtype)
    return jnp.concatenate([w[:, :gdn_main], w[:, gdn_main + ba:], w[:, gdn_main:gdn_main + ba], pad], axis=1)


def kernel(x, c, ada_w, ada_b, norm1_w, norm2_w, w_in, gdn_conv_w, gdn_a_log, gdn_dt_bias, gdn_norm_w, rwkv_mu, rwkv_w0, rwkv_w2, rwkv_a0, rwkv_a2, rwkv_g2, rwkv_k_k, rwkv_k_a, rwkv_r_k, rwkv_ln_w, rwkv_ln_b, rwkv_v0, rwkv_v1, rwkv_v2, w_out, w_up, w_down, final_norm_w):
    bsz, t, _ = x.shape
    depth = ada_w.shape[0]
    tm = min(t, 512)
    tt = min(t, 512)
    mod = _modulation(c, ada_w, ada_b).reshape(depth, bsz, 6, D_MODEL)
    cos_t, sin_t = _rope_tables(t)
    v_first = None
    for l in range(depth):
        p = _projection(x, mod[l], norm1_w[l], _reorder_w_in(w_in[l]).astype(bf16), tm)
        o_a = _gdn(p, gdn_conv_w[l], gdn_a_log[l], gdn_dt_bias[l], gdn_norm_w[l], tt)
        o_b = _retention(p, cos_t, sin_t, tt)
        o_c, v_first = _rwkv(p, v_first, rwkv_mu[l], rwkv_w0[l], rwkv_w2[l], rwkv_a0[l], rwkv_a2[l], rwkv_g2[l],
                             rwkv_k_k[l], rwkv_k_a[l], rwkv_r_k[l], rwkv_ln_w[l], rwkv_ln_b[l],
                             None if l == 0 else rwkv_v0[l - 1], None if l == 0 else rwkv_v1[l - 1],
                             None if l == 0 else rwkv_v2[l - 1], tt)
        x = _out_mlp(x, o_a, o_b, o_c, mod[l], norm2_w[l], w_out[l].astype(bf16), w_up[l].astype(bf16),
                     w_down[l].astype(bf16), final_norm_w, l == depth - 1, tm)
    return x
```

```python
import functools
import math

import jax
import jax.numpy as jnp
from jax import lax
from jax.experimental import pallas as pl
from jax.experimental.pallas import tpu as pltpu

f32 = jnp.float32
bf16 = jnp.bfloat16

D_MODEL = 1024
D_FF = 4 * D_MODEL
HEADS = 4
GDN_D = 128
GDN_W = HEADS * GDN_D
GDN_CONV = 4
HD = 64
HW = HEADS * HD
CHUNK = 64
ROPE_BASE = 10000.0
NORM_EPS = 1e-6
L2_EPS = 1e-6
RET_GN_EPS = 1e-6
RWKV_GN_EPS = 64e-5
RWKV_LORA_W = 128

P_GDN = 0
P_RET = 2048
P_RWKV = 3072
P_BA = 3968
P_COLS = 4096

VMEM_LIMIT = 48 * 1024 * 1024


def _iota(shape, dim):
    return lax.broadcasted_iota(jnp.int32, shape, dim)


def _sigmoid(x):
    return jax.nn.sigmoid(x)


def _silu(x):
    return x * jax.nn.sigmoid(x)


def _softplus(x):
    return jnp.maximum(x, 0.0) + jnp.log1p(jnp.exp(-jnp.abs(x)))


_NN = (((1,), (0,)), ((), ()))
_NT = (((1,), (1,)), ((), ()))
_TN = (((0,), (0,)), ((), ()))


def _dg(a, b, dims):
    return lax.dot_general(a, b, dims, preferred_element_type=f32)


def _mm(a, b, dims=_NN):
    return _dg(a.astype(bf16), b.astype(bf16), dims)


def _split3(x):
    hi = x.astype(bf16)
    r1 = x - hi.astype(f32)
    mid = r1.astype(bf16)
    lo = (r1 - mid.astype(f32)).astype(bf16)
    return hi, mid, lo


def _split2(x):
    hi = x.astype(bf16)
    lo = (x - hi.astype(f32)).astype(bf16)
    return hi, lo


def _mm_lexact(a_bf, b, dims=_NN):
    b0, b1, b2 = _split3(b)
    return _dg(a_bf, b0, dims) + _dg(a_bf, b1, dims) + _dg(a_bf, b2, dims)


def _mm_r16(a, b_bf, dims=_NN):
    a0, a1 = _split2(a)
    return _dg(a0, b_bf, dims) + _dg(a1, b_bf, dims)


def _mm3(a, b, dims=_NN):
    a0, a1 = _split2(a)
    b0, b1 = _split2(b)
    return _dg(a0, b0, dims) + (_dg(a0, b1, dims) + _dg(a1, b0, dims))


def _tri_inv_many(lows, ii, jj):
    x = ii ^ jj
    ts = [jnp.where(x == 0, 1.0, jnp.where(x < 2, low, 0.0)) for low in lows]
    for s in range(1, 6):
        join = (x >> s) == 1
        ys = [_mm(jnp.where(join, low, 0.0), t) for low, t in zip(lows, ts)]
        ts = [t + _mm(t, y) for t, y in zip(ts, ys)]
    return ts


def _expand_heads(x, head_masks):
    return jnp.concatenate([jnp.where(m, x, 0.0) for m in head_masks], axis=0)


def _collapse_heads(x, c):
    return x[0:c] + x[c:2 * c] + x[2 * c:3 * c] + x[3 * c:4 * c]


def _head_masks():
    lane = _iota((1, HW), 1)
    return [(lane // HD) == h for h in range(HEADS)]


def _block_mask():
    return (_iota((HW, HW), 0) // HD) == (_iota((HW, HW), 1) // HD)


def _mod_body(c_ref, w_ref, b_ref, o_ref):
    cond = _silu(c_ref[...])
    o_ref[0] = _mm3(cond, w_ref[0]) + b_ref[0]


def _modulation(c, ada_w, ada_b):
    depth, _, n = ada_w.shape
    bsz = c.shape[0]
    nb = 512
    return pl.pallas_call(
        _mod_body,
        out_shape=jax.ShapeDtypeStruct((depth, bsz, n), f32),
        grid=(depth, n // nb),
        in_specs=[pl.BlockSpec((bsz, D_MODEL), lambda l, j: (0, 0)),
                  pl.BlockSpec((1, D_MODEL, nb), lambda l, j: (l, 0, j)),
                  pl.BlockSpec((1, 1, nb), lambda l, j: (l, 0, j))],
        out_specs=pl.BlockSpec((1, bsz, nb), lambda l, j: (l, 0, j)),
        compiler_params=pltpu.CompilerParams(dimension_semantics=("parallel", "parallel")),
        name="adaln_mod",
    )(c, ada_w, ada_b.reshape(depth, 1, n))


def _rope_body(cos_ref, sin_ref, *, tt):
    shape = (tt, HW)
    pos = (pl.program_id(0) * tt + _iota(shape, 0)).astype(f32)
    lane = _iota(shape, 1)
    half = HD // 2
    inv_freq = jnp.exp((lane % half).astype(f32) * (-math.log(ROPE_BASE) / half))
    ang = pos * inv_freq
    cos_ref[...] = jnp.cos(ang)
    s = jnp.sin(ang)
    sin_ref[...] = jnp.where((lane % HD) < half, -s, s)


def _rope_tables(t):
    tt = min(t, 512)
    return pl.pallas_call(
        functools.partial(_rope_body, tt=tt),
        out_shape=(jax.ShapeDtypeStruct((t, HW), f32), jax.ShapeDtypeStruct((t, HW), f32)),
        grid=(t // tt,),
        out_specs=(pl.BlockSpec((tt, HW), lambda i: (i, 0)), pl.BlockSpec((tt, HW), lambda i: (i, 0))),
        compiler_params=pltpu.CompilerParams(dimension_semantics=("parallel",)),
        name="rope_tables",
    )()


def _proj_body(x_ref, mod_ref, nw_ref, w_ref, p_ref):
    x = x_ref[0]
    m = mod_ref[0]
    y = x * lax.rsqrt(jnp.mean(x * x, axis=-1, keepdims=True) + NORM_EPS) * nw_ref[...]
    h = y * (1.0 + m[1:2]) + m[0:1]
    p_ref[0] = jnp.dot(h.astype(bf16), w_ref[...], preferred_element_type=f32)


def _projection(x, mod, norm_w, w_in_r, tm):
    bsz, t, _ = x.shape
    return pl.pallas_call(
        _proj_body,
        out_shape=jax.ShapeDtypeStruct((bsz, t, P_COLS), f32),
        grid=(bsz, t // tm),
        in_specs=[pl.BlockSpec((1, tm, D_MODEL), lambda b, i: (b, i, 0)),
                  pl.BlockSpec((1, 6, D_MODEL), lambda b, i: (b, 0, 0)),
                  pl.BlockSpec((1, D_MODEL), lambda b, i: (0, 0)),
                  pl.BlockSpec((D_MODEL, P_COLS), lambda b, i: (0, 0), pipeline_mode=pl.Buffered(1))],
        out_specs=pl.BlockSpec((1, tm, P_COLS), lambda b, i: (b, i, 0)),
        compiler_params=pltpu.CompilerParams(dimension_semantics=("parallel", "parallel"),
                                             vmem_limit_bytes=VMEM_LIMIT),
        name="in_proj",
    )(x, mod, norm_w.reshape(1, D_MODEL), w_in_r)


def _gdn_body(qkvz_ref, ba_ref, cw_ref, gp_ref, nw_ref, o_ref, s_ref, carry_ref, *, tt):
    c = CHUNK
    nc = tt // c

    @pl.when(pl.program_id(1) == 0)
    def _():
        s_ref[...] = jnp.zeros_like(s_ref)
        carry_ref[...] = jnp.zeros_like(carry_ref)

    qkv = []
    for gi in range(3):
        cols = slice(gi * GDN_W, (gi + 1) * GDN_W)
        raw = qkvz_ref[0, :, cols]
        xw = jnp.concatenate([carry_ref[:, cols], raw], axis=0)
        cw = cw_ref[:, cols]
        y = (cw[3:4] * raw + cw[2:3] * xw[7:7 + tt] + cw[1:2] * xw[6:6 + tt] + cw[0:1] * xw[5:5 + tt])
        y = _silu(y)
        carry_ref[:, cols] = raw[tt - 8:tt]
        segs = [y[:, h * GDN_D:(h + 1) * GDN_D] for h in range(HEADS)]
        if gi < 2:
            segs = [s * lax.rsqrt(jnp.sum(s * s, axis=-1, keepdims=True) + L2_EPS) for s in segs]
        qkv.append(segs)
    qs, ks, vs = qkv

    ba = ba_ref[0]
    beta_all = _sigmoid(ba)
    gp = gp_ref[...]
    g = -jnp.exp(gp[0:1]) * _softplus(ba + gp[1:2])
    ii, jj = _iota((tt, tt), 0), _iota((tt, tt), 1)
    same = (ii // c) == (jj // c)
    incl = same & (jj <= ii)
    strict = same & (jj < ii)
    cum_mat = jnp.where(incl, 1.0, 0.0).astype(bf16)
    last_mat = jnp.where(jj == (ii // c) * c + (c - 1), 1.0, 0.0).astype(bf16)
    sel = jnp.where(_iota((8, 128), 1) == _iota((8, 128), 0) + HEADS, 1.0, 0.0).astype(bf16)
    gc_all = _mm_lexact(cum_mat, g)
    gl_all = _mm_lexact(last_mat, gc_all)
    gcrows = _mm_lexact(sel, gc_all, _NT)
    scale = GDN_D ** -0.5

    lows, attns, rhss, qdecs, kdecs = [], [], [], [], []
    for h in range(HEADS):
        q = qs[h] * scale
        k = ks[h]
        beta = beta_all[:, h:h + 1]
        gc = gc_all[:, HEADS + h:HEADS + h + 1]
        gl = gl_all[:, HEADS + h:HEADS + h + 1]
        dec = jnp.where(incl, jnp.exp(jnp.where(incl, gc - gcrows[h:h + 1, :], 0.0)), 0.0)
        kb = k * beta
        eg = jnp.exp(gc)
        lows.append(-(_mm(kb, k, _NT) * jnp.where(strict, dec, 0.0)))
        attns.append((_mm(q, k, _NT) * dec).astype(bf16))
        rhss.append(jnp.concatenate([vs[h] * beta, kb * eg], axis=1))
        qdecs.append((q * eg).astype(bf16))
        kdecs.append((k * jnp.exp(gl - gc)).astype(bf16))
    tmats = _tri_inv_many(lows, ii, jj)
    sols = [_mm(tm, rhs) for tm, rhs in zip(tmats, rhss)]

    nw = nw_ref[...]
    for ic in range(nc):
        rows = slice(ic * c, (ic + 1) * c)
        states = [s_ref[h] for h in range(HEADS)]
        ws = [_mm(jnp.concatenate([sols[h][rows, GDN_D:].astype(bf16), qdecs[h][rows]], axis=0), states[h])
              for h in range(HEADS)]
        v_new = [sols[h][rows, :GDN_D] - ws[h][:c] for h in range(HEADS)]
        outs = [ws[h][c:] + _mm(attns[h][rows, ic * c:(ic + 1) * c], v_new[h]) for h in range(HEADS)]
        for h in range(HEADS):
            d = jnp.exp(gl_all[ic * c:ic * c + 1, HEADS + h:HEADS + h + 1])
            s_ref[h] = states[h] * d + _mm(kdecs[h][rows], v_new[h], _TN)
        for h in range(HEADS):
            o = outs[h]
            o = o * lax.rsqrt(jnp.mean(o * o, axis=-1, keepdims=True) + NORM_EPS) * nw
            z = qkvz_ref[0, rows, 3 * GDN_W + h * GDN_D:3 * GDN_W + (h + 1) * GDN_D]
            o_ref[0, rows, h * GDN_D:(h + 1) * GDN_D] = o * _silu(z)


def _gdn(p, conv_w, a_log, dt_bias, norm_w, tt):
    bsz, t, _ = p.shape
    gp = jnp.zeros((8, 128), f32)
    gp = gp.at[0, HEADS:2 * HEADS].set(a_log).at[1, HEADS:2 * HEADS].set(dt_bias)
    return pl.pallas_call(
        functools.partial(_gdn_body, tt=tt),
        out_shape=jax.ShapeDtypeStruct((bsz, t, GDN_W), f32),
        grid=(bsz, t // tt),
        in_specs=[pl.BlockSpec((1, tt, 4 * GDN_W), lambda b, i: (b, i, P_GDN // (4 * GDN_W))),
                  pl.BlockSpec((1, tt, 128), lambda b, i: (b, i, P_BA // 128)),
                  pl.BlockSpec((GDN_CONV, 3 * GDN_W), lambda b, i: (0, 0)),
                  pl.BlockSpec((8, 128), lambda b, i: (0, 0)),
                  pl.BlockSpec((1, GDN_D), lambda b, i: (0, 0))],
        out_specs=pl.BlockSpec((1, tt, GDN_W), lambda b, i: (b, i, 0)),
        scratch_shapes=[pltpu.VMEM((HEADS, GDN_D, GDN_D), f32),
                        pltpu.VMEM((8, 3 * GDN_W), f32)],
        compiler_params=pltpu.CompilerParams(dimension_semantics=("parallel", "arbitrary"),
                                             vmem_limit_bytes=VMEM_LIMIT),
        name="gdn_mixer",
    )(p, p, conv_w, gp, norm_w.reshape(1, GDN_D))


def _seg_stat(x, blk_bf):
    return _mm_r16(x, blk_bf)


def _ret_body(p_ref, cos_ref, sin_ref, o_ref, r_ref, *, tt):
    @pl.when(pl.program_id(1) == 0)
    def _():
        r_ref[...] = jnp.zeros_like(r_ref)

    lane_t = _iota((tt, HW), 1)
    first_half = (lane_t % HD) < (HD // 2)
    cosf = cos_ref[...]
    sins = sin_ref[...]

    def rotary(x):
        swapped = jnp.where(first_half, pltpu.roll(x, HW - HD // 2, 1), pltpu.roll(x, HD // 2, 1))
        return x * cosf + swapped * sins

    q = rotary(p_ref[0, :, 0:HW])
    k = rotary(p_ref[0, :, HW:2 * HW]) * (HD ** -0.5)
    v = p_ref[0, :, 2 * HW:3 * HW]
    gate = p_ref[0, :, 3 * HW:4 * HW]

    hm = _head_masks()
    blk = _block_mask()
    blk_bf = jnp.where(blk, 1.0, 0.0).astype(bf16)
    log_gamma = [math.log(1.0 - 2.0 ** (-5.0 - h)) for h in range(HEADS)]
    lg = jnp.zeros((1, HW), f32)
    for h in range(HEADS):
        lg = jnp.where(hm[h], log_gamma[h], lg)
    pos = _iota((tt, HW), 0).astype(f32)

    state = r_ref[...]
    o = _mm(q * jnp.exp((pos + 1.0) * lg), state)
    rel = (_iota((tt, tt), 0) - _iota((tt, tt), 1)).astype(f32)
    k_bf = k.astype(bf16)
    for h in range(HEADS):
        dmask = jnp.where(rel >= 0, jnp.exp(jnp.maximum(rel, 0.0) * log_gamma[h]), 0.0)
        scores = _dg(jnp.where(hm[h], q, 0.0).astype(bf16), k_bf, _NT) * dmask
        o = o + _mm(scores, jnp.where(hm[h], v, 0.0))
    upd = _mm(k * jnp.exp((tt - 1.0 - pos) * lg), v, _TN)
    r_ref[...] = state * jnp.exp(float(tt) * lg) + jnp.where(blk, upd, 0.0)

    mu = _seg_stat(o, blk_bf) * (1.0 / HD)
    oc = o - mu
    var = _seg_stat(oc * oc, blk_bf) * (1.0 / HD)
    o_ref[0] = oc * lax.rsqrt(var + RET_GN_EPS) * _silu(gate)


def _retention(p, cos_t, sin_t, tt):
    bsz, t, _ = p.shape
    return pl.pallas_call(
        functools.partial(_ret_body, tt=tt),
        out_shape=jax.ShapeDtypeStruct((bsz, t, HW), f32),
        grid=(bsz, t // tt),
        in_specs=[pl.BlockSpec((1, tt, 4 * HW), lambda b, i: (b, i, P_RET // (4 * HW))),
                  pl.BlockSpec((tt, HW), lambda b, i: (i, 0)),
                  pl.BlockSpec((tt, HW), lambda b, i: (i, 0))],
        out_specs=pl.BlockSpec((1, tt, HW), lambda b, i: (b, i, 0)),
        scratch_shapes=[pltpu.VMEM((HW, HW), f32)],
        compiler_params=pltpu.CompilerParams(dimension_semantics=("parallel", "arbitrary"),
                                             vmem_limit_bytes=VMEM_LIMIT),
        name="retention_mixer",
    )(p, cos_t, sin_t)


_RW_W0, _RW_A0, _RW_KK, _RW_KA, _RW_RK, _RW_LNW, _RW_LNB, _RW_V0 = range(8)


def _rwkv_body(*refs, tt, first):
    if first:
        (p_ref, mu_ref, par_ref, w2_ref, a2_ref, g2_ref, o_ref, vf_out_ref, s_ref, carry_ref) = refs
    else:
        (p_ref, vf_ref, mu_ref, par_ref, w2_ref, a2_ref, g2_ref, v1_ref, v2_ref, o_ref, s_ref, carry_ref) = refs
    c = CHUNK
    nc = tt // c

    @pl.when(pl.program_id(1) == 0)
    def _():
        s_ref[...] = jnp.zeros_like(s_ref)
        carry_ref[...] = jnp.zeros_like(carry_ref)

    blk = _block_mask()
    blk_bf = jnp.where(blk, 1.0, 0.0).astype(bf16)
    par = par_ref[...]
    row = lambda i: par[i:i + 1]

    raw = p_ref[0]
    xw = jnp.concatenate([carry_ref[...], raw], axis=0)
    carry_ref[...] = raw[tt - 8:tt]
    pm = raw + (xw[7:7 + tt] - raw) * mu_ref[...]
    r = pm[:, 0:HW]
    k = pm[:, HW:2 * HW]
    v = pm[:, 2 * HW:3 * HW]
    lora = pm[:, 3 * HW:3 * HW + RWKV_LORA_W]
    w_log = -_softplus(-(row(_RW_W0) + _mm(jnp.tanh(lora), w2_ref[...]))) - 0.5
    lw = -jnp.exp(w_log)
    a = _sigmoid(row(_RW_A0) + _mm(lora, a2_ref[...]))
    gate = _mm(_sigmoid(lora), g2_ref[...])
    if first:
        vf_out_ref[0] = v
    else:
        mix = _sigmoid(row(_RW_V0) + _mm(_mm(v, v1_ref[...]), v2_ref[...]))
        v = v + (vf_ref[0] - v) * mix
    kk = k * row(_RW_KK)
    kk = kk * lax.rsqrt(_seg_stat(kk * kk, blk_bf) + L2_EPS)
    k = k * (1.0 + (a - 1.0) * row(_RW_KA))
    a_ = -kk
    b_ = kk * a
    bonus = _seg_stat(r * k * row(_RW_RK), blk_bf) * v

    ii, jj = _iota((tt, tt), 0), _iota((tt, tt), 1)
    cum_mat = jnp.where(((ii // c) == (jj // c)) & (jj <= ii), 1.0, 0.0).astype(bf16)
    last_mat = jnp.where(jj == (ii // c) * c + (c - 1), 1.0, 0.0).astype(bf16)
    gcum = _mm_lexact(cum_mat, lw)
    glast = _mm_lexact(last_mat, gcum)
    e_neg = jnp.exp(-gcum)
    rt = r * jnp.exp(gcum)
    at = a_ * jnp.exp(gcum - lw)
    kt = k * e_neg
    bt = b_ * e_neg
    e_rem = jnp.exp(glast - gcum)
    kd = k * e_rem
    bd = b_ * e_rem

    hm = _head_masks()
    bi, bj = _iota((HW, HW), 0), _iota((HW, HW), 1)
    strict_blk = bi > bj
    incl_side = _iota((c, HW), 0) >= (_iota((c, HW), 1) % c)
    ln_w, ln_b = row(_RW_LNW), row(_RW_LNB)

    a_abs, rhss, a_rbs, ylocs = [], [], [], []
    for ic in range(nc):
        rows = slice(ic * c, (ic + 1) * c)
        ae, ke, be, ve = (_expand_heads(x[rows], hm).astype(bf16) for x in (at, kt, bt, v))
        a_abs.append(jnp.where(strict_blk, _dg(ae, be, _NT), 0.0))
        a_ak = jnp.where(strict_blk, _dg(ae, ke, _NT), 0.0)
        rhss.append(jnp.concatenate([ae.astype(f32), _mm(a_ak, ve)], axis=1))
        rtc = rt[rows].astype(bf16)
        a_rk = jnp.where(incl_side, _dg(rtc, ke, _NT), 0.0)
        a_rbs.append(jnp.where(incl_side, _dg(rtc, be, _NT), 0.0).astype(bf16))
        ylocs.append(_mm(a_rk, ve))
    tmats = _tri_inv_many(a_abs, bi, bj)
    wus = [_collapse_heads(_mm(tm, rhs), c) for tm, rhs in zip(tmats, rhss)]

    for ic in range(nc):
        rows = slice(ic * c, (ic + 1) * c)
        state = s_ref[...]
        ws = _mm(jnp.concatenate([wus[ic][:, :HW], rt[rows]], axis=0), state, _NT)
        u = wus[ic][:, HW:] + ws[:c]
        y = ws[c:] + ylocs[ic] + _mm(a_rbs[ic], _expand_heads(u, hm))
        upd = _mm(jnp.concatenate([v[rows], u], axis=0), jnp.concatenate([kd[rows], bd[rows]], axis=0), _TN)
        s_ref[...] = state * jnp.exp(glast[ic * c:ic * c + 1, :]) + jnp.where(blk, upd, 0.0)
        mu = _seg_stat(y, blk_bf) * (1.0 / HD)
        yc = y - mu
        var = _seg_stat(yc * yc, blk_bf) * (1.0 / HD)
        yn = yc * lax.rsqrt(var + RWKV_GN_EPS) * ln_w + ln_b
        o_ref[0, rows, :] = (yn + bonus[rows]) * gate[rows]


def _pad_rows(w, start, total):
    return jnp.zeros((total, w.shape[1]), f32).at[start:start + w.shape[0]].set(w)


def _rwkv(p, v_first, mu, w0, w2, a0, a2, g2, k_k, k_a, r_k, ln_w, ln_b, v0, v1, v2, tt):
    bsz, t, _ = p.shape
    first = v_first is None
    mu_pad = jnp.zeros((1, 4 * HW), f32).at[0, :mu.shape[0]].set(mu)
    par = jnp.stack([w0, a0, k_k, k_a, r_k.reshape(HW), ln_w, ln_b,
                     jnp.zeros((HW,), f32) if first else v0], axis=0)
    w2p = _pad_rows(w2, 0, RWKV_LORA_W).astype(bf16)
    a2p = _pad_rows(a2, 32, RWKV_LORA_W).astype(bf16)
    g2p = _pad_rows(g2, 64, RWKV_LORA_W).astype(bf16)
    full = lambda shape: pl.BlockSpec(shape, lambda b, i: (0,) * len(shape))
    tile = pl.BlockSpec((1, tt, HW), lambda b, i: (b, i, 0))
    p_spec = pl.BlockSpec((1, tt, 4 * HW), lambda b, i: (b, i, P_RWKV // (4 * HW)))
    common = [full((1, 4 * HW)), full((8, HW)), full((RWKV_LORA_W, HW)), full((RWKV_LORA_W, HW)),
              full((RWKV_LORA_W, HW))]
    if first:
        in_specs = [p_spec] + common
        args = (p, mu_pad, par, w2p, a2p, g2p)
        out_shape = (jax.ShapeDtypeStruct((bsz, t, HW), f32), jax.ShapeDtypeStruct((bsz, t, HW), f32))
        out_specs = (tile, tile)
    else:
        v1p = jnp.zeros((HW, 128), f32).at[:, :v1.shape[1]].set(v1).astype(bf16)
        v2p = _pad_rows(v2, 0, 128).astype(bf16)
        in_specs = [p_spec, tile] + common + [full((HW, 128)), full((128, HW))]
        args = (p, v_first, mu_pad, par, w2p, a2p, g2p, v1p, v2p)
        out_shape = jax.ShapeDtypeStruct((bsz, t, HW), f32)
        out_specs = tile
    res = pl.pallas_call(
        functools.partial(_rwkv_body, tt=tt, first=first),
        out_shape=out_shape,
        grid=(bsz, t // tt),
        in_specs=in_specs,
        out_specs=out_specs,
        scratch_shapes=[pltpu.VMEM((HW, HW), f32),
                        pltpu.VMEM((8, 4 * HW), f32)],
        compiler_params=pltpu.CompilerParams(dimension_semantics=("parallel", "arbitrary"),
                                             vmem_limit_bytes=VMEM_LIMIT),
        name="rwkv7_mixer_first" if first else "rwkv7_mixer",
    )(*args)
    return res if first else (res, v_first)


def _mlp_body(x_ref, oa_ref, ob_ref, oc_ref, mod_ref, nw_ref, wo_ref, wu_ref, wd_ref, fw_ref, o_ref, *, last):
    m = mod_ref[0]
    mix = (jnp.dot(oa_ref[0].astype(bf16), wo_ref[0:GDN_W, :], preferred_element_type=f32)
           + jnp.dot(ob_ref[0].astype(bf16), wo_ref[GDN_W:GDN_W + HW, :], preferred_element_type=f32)
           + jnp.dot(oc_ref[0].astype(bf16), wo_ref[GDN_W + HW:, :], preferred_element_type=f32))
    x = x_ref[0] + m[2:3] * mix
    y = x * lax.rsqrt(jnp.mean(x * x, axis=-1, keepdims=True) + NORM_EPS) * nw_ref[...]
    h = (y * (1.0 + m[4:5]) + m[3:4]).astype(bf16)
    acc = jnp.zeros(x.shape, f32)
    nb = 1024
    for j in range(D_FF // nb):
        hid = jnp.dot(h, wu_ref[:, j * nb:(j + 1) * nb], preferred_element_type=f32)
        hid = jnp.square(jnp.maximum(hid, 0.0))
        acc = acc + jnp.dot(hid.astype(bf16), wd_ref[j * nb:(j + 1) * nb, :], preferred_element_type=f32)
    x = x + m[5:6] * acc
    if last:
        x = x * lax.rsqrt(jnp.mean(x * x, axis=-1, keepdims=True) + NORM_EPS) * fw_ref[...]
    o_ref[0] = x


def _out_mlp(x, oa, ob, oc, mod, norm_w, w_out, w_up, w_down, final_w, last, tm):
    bsz, t, _ = x.shape
    tile = lambda w: pl.BlockSpec((1, tm, w), lambda b, i: (b, i, 0))
    const = lambda shape: pl.BlockSpec(shape, lambda b, i: (0, 0), pipeline_mode=pl.Buffered(1))
    return pl.pallas_call(
        functools.partial(_mlp_body, last=last),
        out_shape=jax.ShapeDtypeStruct((bsz, t, D_MODEL), f32),
        grid=(bsz, t // tm),
        in_specs=[tile(D_MODEL), tile(GDN_W), tile(HW), tile(HW),
                  pl.BlockSpec((1, 6, D_MODEL), lambda b, i: (b, 0, 0)),
                  pl.BlockSpec((1, D_MODEL), lambda b, i: (0, 0)),
                  const((D_MODEL, D_MODEL)), const((D_MODEL, D_FF)), const((D_FF, D_MODEL)),
                  pl.BlockSpec((1, D_MODEL), lambda b, i: (0, 0))],
        out_specs=tile(D_MODEL),
        compiler_params=pltpu.CompilerParams(dimension_semantics=("parallel", "parallel"),
                                             vmem_limit_bytes=VMEM_LIMIT),
        name="out_mlp_last" if last else "out_mlp",
    )(x, oa, ob, oc, mod, norm_w.reshape(1, D_MODEL), w_out, w_up, w_down, final_w.reshape(1, D_MODEL))


def _reorder_w_in(w):
    gdn_main = 4 * GDN_W
    ba = 2 * HEADS
    pad = jnp.zeros((w.shape[0], P_COLS - w.shape[1]), w.dtype)
    return jnp.concatenate([w[:, :gdn_main], w[:, gdn_main + ba:], w[:, gdn_main:gdn_main + ba], pad], axis=1)


def kernel(x, c, ada_w, ada_b, norm1_w, norm2_w, w_in, gdn_conv_w, gdn_a_log, gdn_dt_bias, gdn_norm_w, rwkv_mu, rwkv_w0, rwkv_w2, rwkv_a0, rwkv_a2, rwkv_g2, rwkv_k_k, rwkv_k_a, rwkv_r_k, rwkv_ln_w, rwkv_ln_b, rwkv_v0, rwkv_v1, rwkv_v2, w_out, w_up, w_down, final_norm_w):
    bsz, t, _ = x.shape
    depth = ada_w.shape[0]
    tm = min(t, 512)
    tt = min(t, 256)
    mod = _modulation(c, ada_w, ada_b).reshape(depth, bsz, 6, D_MODEL)
    cos_t, sin_t = _rope_tables(t)
    v_first = None
    for l in range(depth):
        p = _projection(x, mod[l], norm1_w[l], _reorder_w_in(w_in[l]).astype(bf16), tm)
        o_a = _gdn(p, gdn_conv_w[l], gdn_a_log[l], gdn_dt_bias[l], gdn_norm_w[l], tt)
        o_b = _retention(p, cos_t, sin_t, tt)
        o_c, v_first = _rwkv(p, v_first, rwkv_mu[l], rwkv_w0[l], rwkv_w2[l], rwkv_a0[l], rwkv_a2[l], rwkv_g2[l],
                             rwkv_k_k[l], rwkv_k_a[l], rwkv_r_k[l], rwkv_ln_w[l], rwkv_ln_b[l],
                             None if l == 0 else rwkv_v0[l - 1], None if l == 0 else rwkv_v1[l - 1],
                             None if l == 0 else rwkv_v2[l - 1], tt)
        x = _out_mlp(x, o_a, o_b, o_c, mod[l], norm2_w[l], w_out[l].astype(bf16), w_up[l].astype(bf16),
                     w_down[l].astype(bf16), final_norm_w, l == depth - 1, tm)
    return x
```

```python
import functools
import math

import jax
import jax.numpy as jnp
from jax import lax
from jax.experimental import pallas as pl
from jax.experimental.pallas import tpu as pltpu

f32 = jnp.float32
bf16 = jnp.bfloat16

D_MODEL = 1024
D_FF = 4 * D_MODEL
HEADS = 4
GDN_D = 128
GDN_W = HEADS * GDN_D
GDN_CONV = 4
HD = 64
HW = HEADS * HD
CHUNK = 64
ROPE_BASE = 10000.0
NORM_EPS = 1e-6
L2_EPS = 1e-6
RET_GN_EPS = 1e-6
RWKV_GN_EPS = 64e-5
RWKV_LORA_W = 128

P_GDN = 0
P_RET = 2048
P_RWKV = 3072
P_BA = 3968
P_COLS = 4096

VMEM_LIMIT = 48 * 1024 * 1024


def _iota(shape, dim):
    return lax.broadcasted_iota(jnp.int32, shape, dim)


def _sigmoid(x):
    return jax.nn.sigmoid(x)


def _silu(x):
    return x * jax.nn.sigmoid(x)


def _softplus(x):
    return jnp.maximum(x, 0.0) + jnp.log1p(jnp.exp(-jnp.abs(x)))


_NN = (((1,), (0,)), ((), ()))
_NT = (((1,), (1,)), ((), ()))
_TN = (((0,), (0,)), ((), ()))


def _dg(a, b, dims):
    return lax.dot_general(a, b, dims, preferred_element_type=f32)


def _mm(a, b, dims=_NN):
    return _dg(a.astype(bf16), b.astype(bf16), dims)


def _split3(x):
    hi = x.astype(bf16)
    r1 = x - hi.astype(f32)
    mid = r1.astype(bf16)
    lo = (r1 - mid.astype(f32)).astype(bf16)
    return hi, mid, lo


def _split2(x):
    hi = x.astype(bf16)
    lo = (x - hi.astype(f32)).astype(bf16)
    return hi, lo


def _mm_lexact(a_bf, b, dims=_NN):
    b0, b1, b2 = _split3(b)
    return _dg(a_bf, b0, dims) + _dg(a_bf, b1, dims) + _dg(a_bf, b2, dims)


def _mm_r16(a, b_bf, dims=_NN):
    a0, a1 = _split2(a)
    return _dg(a0, b_bf, dims) + _dg(a1, b_bf, dims)


def _mm3(a, b, dims=_NN):
    a0, a1 = _split2(a)
    b0, b1 = _split2(b)
    return _dg(a0, b0, dims) + (_dg(a0, b1, dims) + _dg(a1, b0, dims))


def _tri_inv_many(lows, ii, jj):
    x = ii ^ jj
    lbs = [low.astype(bf16) for low in lows]
    base = jnp.where(x < 2, 1.0, 0.0).astype(bf16)
    eye = jnp.where(x == 0, 1.0, 0.0).astype(bf16)
    ts = [lb * base + eye for lb in lbs]
    for s in range(1, 6):
        join = jnp.where((x >> s) == 1, 1.0, 0.0).astype(bf16)
        ys = [_dg(lb, t, _NN).astype(bf16) for lb, t in zip(lbs, ts)]
        ts = [t + _dg(t, y, _NN).astype(bf16) * join for t, y in zip(ts, ys)]
    return ts


def _expand_heads(x, head_masks):
    return jnp.concatenate([jnp.where(m, x, 0.0) for m in head_masks], axis=0)


def _collapse_heads(x, c):
    return x[0:c] + x[c:2 * c] + x[2 * c:3 * c] + x[3 * c:4 * c]


def _head_masks():
    lane = _iota((1, HW), 1)
    return [(lane // HD) == h for h in range(HEADS)]


def _block_mask():
    return (_iota((HW, HW), 0) // HD) == (_iota((HW, HW), 1) // HD)


def _mod_body(c_ref, w_ref, b_ref, o_ref):
    cond = _silu(c_ref[...])
    o_ref[0] = _mm3(cond, w_ref[0]) + b_ref[0]


def _modulation(c, ada_w, ada_b):
    depth, _, n = ada_w.shape
    bsz = c.shape[0]
    nb = 512
    return pl.pallas_call(
        _mod_body,
        out_shape=jax.ShapeDtypeStruct((depth, bsz, n), f32),
        grid=(depth, n // nb),
        in_specs=[pl.BlockSpec((bsz, D_MODEL), lambda l, j: (0, 0)),
                  pl.BlockSpec((1, D_MODEL, nb), lambda l, j: (l, 0, j)),
                  pl.BlockSpec((1, 1, nb), lambda l, j: (l, 0, j))],
        out_specs=pl.BlockSpec((1, bsz, nb), lambda l, j: (l, 0, j)),
        compiler_params=pltpu.CompilerParams(dimension_semantics=("parallel", "parallel")),
        name="adaln_mod",
    )(c, ada_w, ada_b.reshape(depth, 1, n))


def _rope_body(cos_ref, sin_ref, *, tt):
    shape = (tt, HW)
    pos = (pl.program_id(0) * tt + _iota(shape, 0)).astype(f32)
    lane = _iota(shape, 1)
    half = HD // 2
    inv_freq = jnp.exp((lane % half).astype(f32) * (-math.log(ROPE_BASE) / half))
    ang = pos * inv_freq
    cos_ref[...] = jnp.cos(ang)
    s = jnp.sin(ang)
    sin_ref[...] = jnp.where((lane % HD) < half, -s, s)


def _rope_tables(t):
    tt = min(t, 512)
    return pl.pallas_call(
        functools.partial(_rope_body, tt=tt),
        out_shape=(jax.ShapeDtypeStruct((t, HW), f32), jax.ShapeDtypeStruct((t, HW), f32)),
        grid=(t // tt,),
        out_specs=(pl.BlockSpec((tt, HW), lambda i: (i, 0)), pl.BlockSpec((tt, HW), lambda i: (i, 0))),
        compiler_params=pltpu.CompilerParams(dimension_semantics=("parallel",)),
        name="rope_tables",
    )()


def _proj_body(x_ref, mod_ref, nw_ref, w_ref, p_ref):
    x = x_ref[0]
    m = mod_ref[0]
    y = x * lax.rsqrt(jnp.mean(x * x, axis=-1, keepdims=True) + NORM_EPS) * nw_ref[...]
    h = y * (1.0 + m[1:2]) + m[0:1]
    p_ref[0] = jnp.dot(h.astype(bf16), w_ref[...], preferred_element_type=f32)


def _projection(x, mod, norm_w, w_in_r, tm):
    bsz, t, _ = x.shape
    return pl.pallas_call(
        _proj_body,
        out_shape=jax.ShapeDtypeStruct((bsz, t, P_COLS), f32),
        grid=(bsz, t // tm),
        in_specs=[pl.BlockSpec((1, tm, D_MODEL), lambda b, i: (b, i, 0)),
                  pl.BlockSpec((1, 6, D_MODEL), lambda b, i: (b, 0, 0)),
                  pl.BlockSpec((1, D_MODEL), lambda b, i: (0, 0)),
                  pl.BlockSpec((D_MODEL, P_COLS), lambda b, i: (0, 0), pipeline_mode=pl.Buffered(1))],
        out_specs=pl.BlockSpec((1, tm, P_COLS), lambda b, i: (b, i, 0)),
        compiler_params=pltpu.CompilerParams(dimension_semantics=("parallel", "parallel"),
                                             vmem_limit_bytes=VMEM_LIMIT),
        name="in_proj",
    )(x, mod, norm_w.reshape(1, D_MODEL), w_in_r)


def _gdn_body(qkvz_ref, ba_ref, cw_ref, gp_ref, nw_ref, o_ref, s_ref, carry_ref, *, tt, nb):
    c = CHUNK
    nc = tt // c

    @pl.when(pl.program_id(1) == 0)
    def _():
        s_ref[...] = jnp.zeros_like(s_ref)
        carry_ref[...] = jnp.zeros_like(carry_ref)

    gp = gp_ref[...]
    ii, jj = _iota((tt, tt), 0), _iota((tt, tt), 1)
    same = (ii // c) == (jj // c)
    incl = same & (jj <= ii)
    strict = same & (jj < ii)
    cum_mat = jnp.where(incl, 1.0, 0.0).astype(bf16)
    sel = jnp.where(_iota((8, 128), 1) == _iota((8, 128), 0) + HEADS, 1.0, 0.0).astype(bf16)
    scale = GDN_D ** -0.5

    lows, attns, rhss, qdecs, kdecs, gls = [], [], [], [], [], []
    for bb in range(nb):
        qkv = []
        for gi in range(3):
            cols = slice(gi * GDN_W, (gi + 1) * GDN_W)
            raw = qkvz_ref[bb, :, cols]
            xw = jnp.concatenate([carry_ref[bb, :, cols], raw], axis=0)
            cw = cw_ref[:, cols]
            y = (cw[3:4] * raw + cw[2:3] * xw[7:7 + tt] + cw[1:2] * xw[6:6 + tt] + cw[0:1] * xw[5:5 + tt])
            y = _silu(y)
            carry_ref[bb, :, cols] = raw[tt - 8:tt]
            segs = [y[:, h * GDN_D:(h + 1) * GDN_D] for h in range(HEADS)]
            if gi < 2:
                segs = [s * lax.rsqrt(jnp.sum(s * s, axis=-1, keepdims=True) + L2_EPS) for s in segs]
            qkv.append(segs)
        qs, ks, vs = qkv

        ba = ba_ref[bb]
        beta_all = _sigmoid(ba)
        g = -jnp.exp(gp[0:1]) * _softplus(ba + gp[1:2])
        gc_all = _mm_lexact(cum_mat, g)
        gl_all = jnp.concatenate(
            [jnp.broadcast_to(gc_all[(ic + 1) * c - 1:(ic + 1) * c], (c, 128)) for ic in range(nc)], axis=0)
        gcrows = _mm_lexact(sel, gc_all, _NT)
        gls.append(gl_all)
        for h in range(HEADS):
            q = qs[h] * scale
            k = ks[h]
            beta = beta_all[:, h:h + 1]
            gc = gc_all[:, HEADS + h:HEADS + h + 1]
            gl = gl_all[:, HEADS + h:HEADS + h + 1]
            dec = jnp.where(incl, jnp.exp(jnp.where(incl, gc - gcrows[h:h + 1, :], 0.0)), 0.0)
            kb = k * beta
            eg = jnp.exp(gc)
            lows.append(-(_mm(kb, k, _NT) * jnp.where(strict, dec, 0.0)))
            attns.append((_mm(q, k, _NT) * dec).astype(bf16))
            rhss.append(jnp.concatenate([vs[h] * beta, kb * eg], axis=1))
            qdecs.append((q * eg).astype(bf16))
            kdecs.append((k * jnp.exp(gl - gc)).astype(bf16))
    tmats = _tri_inv_many(lows, ii, jj)
    sols = [_dg(tm, rhs.astype(bf16), _NN) for tm, rhs in zip(tmats, rhss)]

    chains = [(bb, h) for bb in range(nb) for h in range(HEADS)]
    rws = lambda ic: slice(ic * c, (ic + 1) * c)
    wqs = [[jnp.concatenate([sols[n][rws(ic), GDN_D:].astype(bf16), qdecs[n][rws(ic)]], axis=0) for ic in range(nc)]
           for n in range(len(chains))]
    cur = [s_ref[bb, h] for bb, h in chains]
    v_new = [[None] * nc for _ in chains]
    cross = [[None] * nc for _ in chains]
    for ic in range(nc):
        ws = [_dg(wqs[n][ic], cur[n].astype(bf16), _NN) for n in range(len(chains))]
        for n, (bb, h) in enumerate(chains):
            v_new[n][ic] = (sols[n][rws(ic), :GDN_D] - ws[n][:c]).astype(bf16)
            cross[n][ic] = ws[n][c:]
            d = jnp.exp(gls[bb][ic * c:ic * c + 1, HEADS + h:HEADS + h + 1])
            cur[n] = cur[n] * d + _dg(kdecs[n][rws(ic)], v_new[n][ic], _TN)
    for n, (bb, h) in enumerate(chains):
        s_ref[bb, h] = cur[n]

    nw = nw_ref[...]
    items = [(n, ic) for ic in range(nc) for n in range(len(chains))]
    outs = [cross[n][ic] + _dg(attns[n][rws(ic), ic * c:(ic + 1) * c], v_new[n][ic], _NN) for n, ic in items]
    for (n, ic), o in zip(items, outs):
        bb, h = chains[n]
        o = o * lax.rsqrt(jnp.mean(o * o, axis=-1, keepdims=True) + NORM_EPS) * nw
        z = qkvz_ref[bb, rws(ic), 3 * GDN_W + h * GDN_D:3 * GDN_W + (h + 1) * GDN_D]
        o_ref[bb, rws(ic), h * GDN_D:(h + 1) * GDN_D] = o * _silu(z)


def _gdn(p, conv_w, a_log, dt_bias, norm_w, tt, nb):
    bsz, t, _ = p.shape
    gp = jnp.zeros((8, 128), f32)
    gp = gp.at[0, HEADS:2 * HEADS].set(a_log).at[1, HEADS:2 * HEADS].set(dt_bias)
    return pl.pallas_call(
        functools.partial(_gdn_body, tt=tt, nb=nb),
        out_shape=jax.ShapeDtypeStruct((bsz, t, GDN_W), f32),
        grid=(bsz // nb, t // tt),
        in_specs=[pl.BlockSpec((nb, tt, 4 * GDN_W), lambda b, i: (b, i, P_GDN // (4 * GDN_W))),
                  pl.BlockSpec((nb, tt, 128), lambda b, i: (b, i, P_BA // 128)),
                  pl.BlockSpec((GDN_CONV, 3 * GDN_W), lambda b, i: (0, 0)),
                  pl.BlockSpec((8, 128), lambda b, i: (0, 0)),
                  pl.BlockSpec((1, GDN_D), lambda b, i: (0, 0))],
        out_specs=pl.BlockSpec((nb, tt, GDN_W), lambda b, i: (b, i, 0)),
        scratch_shapes=[pltpu.VMEM((nb, HEADS, GDN_D, GDN_D), f32),
                        pltpu.VMEM((nb, 8, 3 * GDN_W), f32)],
        compiler_params=pltpu.CompilerParams(dimension_semantics=("parallel", "arbitrary"),
                                             vmem_limit_bytes=VMEM_LIMIT),
        name="gdn_mixer",
    )(p, p, conv_w, gp, norm_w.reshape(1, GDN_D))


def _seg_stat(x, blk_bf):
    return _mm_r16(x, blk_bf)


def _ret_body(p_ref, cos_ref, sin_ref, o_ref, r_ref, *, tt):
    @pl.when(pl.program_id(1) == 0)
    def _():
        r_ref[...] = jnp.zeros_like(r_ref)

    lane_t = _iota((tt, HW), 1)
    first_half = (lane_t % HD) < (HD // 2)
    cosf = cos_ref[...]
    sins = sin_ref[...]

    def rotary(x):
        swapped = jnp.where(first_half, pltpu.roll(x, HW - HD // 2, 1), pltpu.roll(x, HD // 2, 1))
        return x * cosf + swapped * sins

    q = rotary(p_ref[0, :, 0:HW])
    k = rotary(p_ref[0, :, HW:2 * HW]) * (HD ** -0.5)
    v = p_ref[0, :, 2 * HW:3 * HW]
    gate = p_ref[0, :, 3 * HW:4 * HW]

    hm = _head_masks()
    blk = _block_mask()
    blk_bf = jnp.where(blk, 1.0, 0.0).astype(bf16)
    log_gamma = [math.log(1.0 - 2.0 ** (-5.0 - h)) for h in range(HEADS)]
    lg = jnp.zeros((1, HW), f32)
    for h in range(HEADS):
        lg = jnp.where(hm[h], log_gamma[h], lg)
    pos = _iota((tt, HW), 0).astype(f32)

    state = r_ref[...]
    o = _mm(q * jnp.exp((pos + 1.0) * lg), state)
    rel = (_iota((tt, tt), 0) - _iota((tt, tt), 1)).astype(f32)
    k_bf = k.astype(bf16)
    for h in range(HEADS):
        dmask = jnp.where(rel >= 0, jnp.exp(jnp.maximum(rel, 0.0) * log_gamma[h]), 0.0)
        scores = _dg(jnp.where(hm[h], q, 0.0).astype(bf16), k_bf, _NT) * dmask
        o = o + _mm(scores, jnp.where(hm[h], v, 0.0))
    upd = _mm(k * jnp.exp((tt - 1.0 - pos) * lg), v, _TN)
    r_ref[...] = state * jnp.exp(float(tt) * lg) + jnp.where(blk, upd, 0.0)

    mu = _seg_stat(o, blk_bf) * (1.0 / HD)
    oc = o - mu
    var = _seg_stat(oc * oc, blk_bf) * (1.0 / HD)
    o_ref[0] = oc * lax.rsqrt(var + RET_GN_EPS) * _silu(gate)


def _retention(p, cos_t, sin_t, tt):
    bsz, t, _ = p.shape
    return pl.pallas_call(
        functools.partial(_ret_body, tt=tt),
        out_shape=jax.ShapeDtypeStruct((bsz, t, HW), f32),
        grid=(bsz, t // tt),
        in_specs=[pl.BlockSpec((1, tt, 4 * HW), lambda b, i: (b, i, P_RET // (4 * HW))),
                  pl.BlockSpec((tt, HW), lambda b, i: (i, 0)),
                  pl.BlockSpec((tt, HW), lambda b, i: (i, 0))],
        out_specs=pl.BlockSpec((1, tt, HW), lambda b, i: (b, i, 0)),
        scratch_shapes=[pltpu.VMEM((HW, HW), f32)],
        compiler_params=pltpu.CompilerParams(dimension_semantics=("parallel", "arbitrary"),
                                             vmem_limit_bytes=VMEM_LIMIT),
        name="retention_mixer",
    )(p, cos_t, sin_t)


_RW_W0, _RW_A0, _RW_KK, _RW_KA, _RW_RK, _RW_LNW, _RW_LNB, _RW_V0 = range(8)


def _rwkv_body(*refs, tt, nb, first):
    if first:
        (p_ref, mu_ref, par_ref, w2_ref, a2_ref, g2_ref, o_ref, vf_out_ref, s_ref, carry_ref) = refs
    else:
        (p_ref, vf_ref, mu_ref, par_ref, w2_ref, a2_ref, g2_ref, v1_ref, v2_ref, o_ref, s_ref, carry_ref) = refs
    c = CHUNK
    nc = tt // c

    @pl.when(pl.program_id(1) == 0)
    def _():
        s_ref[...] = jnp.zeros_like(s_ref)
        carry_ref[...] = jnp.zeros_like(carry_ref)

    blk = _block_mask()
    blk_bf = jnp.where(blk, 1.0, 0.0).astype(bf16)
    par = par_ref[...]
    row = lambda i: par[i:i + 1]
    hm = _head_masks()
    ii, jj = _iota((tt, tt), 0), _iota((tt, tt), 1)
    cum_mat = jnp.where(((ii // c) == (jj // c)) & (jj <= ii), 1.0, 0.0).astype(bf16)
    bi, bj = _iota((HW, HW), 0), _iota((HW, HW), 1)
    strict_blk = bi > bj
    incl_side = _iota((c, HW), 0) >= (_iota((c, HW), 1) % c)
    ln_w, ln_b = row(_RW_LNW), row(_RW_LNB)

    a_abs, aes, akvs, a_rbs, ylocs = [], [], [], [], []
    per_row = []
    for bb in range(nb):
        raw = p_ref[bb]
        xw = jnp.concatenate([carry_ref[bb], raw], axis=0)
        carry_ref[bb] = raw[tt - 8:tt]
        pm = raw + (xw[7:7 + tt] - raw) * mu_ref[...]
        r = pm[:, 0:HW]
        k = pm[:, HW:2 * HW]
        v = pm[:, 2 * HW:3 * HW]
        lora = pm[:, 3 * HW:3 * HW + RWKV_LORA_W]
        w_log = -_softplus(-(row(_RW_W0) + _mm(jnp.tanh(lora), w2_ref[...]))) - 0.5
        lw = -jnp.exp(w_log)
        a = _sigmoid(row(_RW_A0) + _mm(lora, a2_ref[...]))
        gate = _mm(_sigmoid(lora), g2_ref[...])
        if first:
            vf_out_ref[bb] = v
        else:
            mix = _sigmoid(row(_RW_V0) + _mm(_mm(v, v1_ref[...]), v2_ref[...]))
            v = v + (vf_ref[bb] - v) * mix
        kk = k * row(_RW_KK)
        kk = kk * lax.rsqrt(_seg_stat(kk * kk, blk_bf) + L2_EPS)
        k = k * (1.0 + (a - 1.0) * row(_RW_KA))
        a_ = -kk
        b_ = kk * a
        bonus = _seg_stat(r * k * row(_RW_RK), blk_bf) * v

        gcum = _mm_lexact(cum_mat, lw)
        glast = jnp.concatenate(
            [jnp.broadcast_to(gcum[(ic + 1) * c - 1:(ic + 1) * c], (c, HW)) for ic in range(nc)], axis=0)
        e_neg = jnp.exp(-gcum)
        rt = (r * jnp.exp(gcum)).astype(bf16)
        at = a_ * jnp.exp(gcum - lw)
        kt = k * e_neg
        bt = b_ * e_neg
        e_rem = jnp.exp(glast - gcum)
        kd = (k * e_rem).astype(bf16)
        bd = (b_ * e_rem).astype(bf16)
        per_row.append((rt, v, kd, bd, glast, bonus, gate))

        for ic in range(nc):
            rows = slice(ic * c, (ic + 1) * c)
            ae, ke, be, ve = (_expand_heads(x[rows], hm).astype(bf16) for x in (at, kt, bt, v))
            a_abs.append(jnp.where(strict_blk, _dg(ae, be, _NT), 0.0))
            a_ak = jnp.where(strict_blk, _dg(ae, ke, _NT), 0.0)
            aes.append(ae)
            akvs.append(_mm(a_ak, ve))
            a_rk = jnp.where(incl_side, _dg(rt[rows], ke, _NT), 0.0)
            a_rbs.append(jnp.where(incl_side, _dg(rt[rows], be, _NT), 0.0).astype(bf16))
            ylocs.append(_mm(a_rk, ve))
    tmats = _tri_inv_many(a_abs, bi, bj)
    wps = [_collapse_heads(_dg(tm, ae, _NN), c) for tm, ae in zip(tmats, aes)]
    ups = [_collapse_heads(_dg(tm, akv.astype(bf16), _NN), c) for tm, akv in zip(tmats, akvs)]

    pcs, ncs = [], []
    for bb in range(nb):
        rt, v, kd, bd, glast, bonus, gate = per_row[bb]
        for ic in range(nc):
            rows = slice(ic * c, (ic + 1) * c)
            n = bb * nc + ic
            pcs.append(jnp.where(blk, _mm(wps[n], bd[rows], _TN), 0.0).astype(bf16))
            ncs.append(jnp.where(blk, _mm(jnp.concatenate([v[rows], ups[n]], axis=0),
                                          jnp.concatenate([kd[rows], bd[rows]], axis=0), _TN), 0.0))

    states = [None] * (nb * nc)
    cur = [s_ref[bb] for bb in range(nb)]
    for ic in range(nc):
        for bb in range(nb):
            n = bb * nc + ic
            states[n] = cur[bb].astype(bf16)
            e_last = jnp.exp(per_row[bb][4][ic * c:ic * c + 1, :])
            cur[bb] = cur[bb] * e_last + _dg(states[n], pcs[n], _NN) + ncs[n]
    for bb in range(nb):
        s_ref[bb] = cur[bb]

    rws = lambda n: slice((n % nc) * c, (n % nc + 1) * c)
    rts = [per_row[n // nc][0][rws(n)] for n in range(nb * nc)]
    wss = [_dg(jnp.concatenate([wps[n].astype(bf16), rts[n]], axis=0), states[n], _NT) for n in range(nb * nc)]
    us = [ups[n] + wss[n][:c] for n in range(nb * nc)]
    ys = [wss[n][c:] + ylocs[n] + _mm(a_rbs[n], _expand_heads(us[n], hm)) for n in range(nb * nc)]
    for bb in range(nb):
        rt, v, kd, bd, glast, bonus, gate = per_row[bb]
        y = jnp.concatenate(ys[bb * nc:(bb + 1) * nc], axis=0)
        mu = _seg_stat(y, blk_bf) * (1.0 / HD)
        yc = y - mu
        var = _seg_stat(yc * yc, blk_bf) * (1.0 / HD)
        yn = yc * lax.rsqrt(var + RWKV_GN_EPS) * ln_w + ln_b
        o_ref[bb] = (yn + bonus) * gate


def _pad_rows(w, start, total):
    return jnp.zeros((total, w.shape[1]), f32).at[start:start + w.shape[0]].set(w)


def _rwkv(p, v_first, mu, w0, w2, a0, a2, g2, k_k, k_a, r_k, ln_w, ln_b, v0, v1, v2, tt, nb):
    bsz, t, _ = p.shape
    first = v_first is None
    mu_pad = jnp.zeros((1, 4 * HW), f32).at[0, :mu.shape[0]].set(mu)
    par = jnp.stack([w0, a0, k_k, k_a, r_k.reshape(HW), ln_w, ln_b,
                     jnp.zeros((HW,), f32) if first else v0], axis=0)
    w2p = _pad_rows(w2, 0, RWKV_LORA_W).astype(bf16)
    a2p = _pad_rows(a2, 32, RWKV_LORA_W).astype(bf16)
    g2p = _pad_rows(g2, 64, RWKV_LORA_W).astype(bf16)
    full = lambda shape: pl.BlockSpec(shape, lambda b, i: (0,) * len(shape))
    tile = pl.BlockSpec((nb, tt, HW), lambda b, i: (b, i, 0))
    p_spec = pl.BlockSpec((nb, tt, 4 * HW), lambda b, i: (b, i, P_RWKV // (4 * HW)))
    common = [full((1, 4 * HW)), full((8, HW)), full((RWKV_LORA_W, HW)), full((RWKV_LORA_W, HW)),
              full((RWKV_LORA_W, HW))]
    if first:
        in_specs = [p_spec] + common
        args = (p, mu_pad, par, w2p, a2p, g2p)
        out_shape = (jax.ShapeDtypeStruct((bsz, t, HW), f32), jax.ShapeDtypeStruct((bsz, t, HW), f32))
        out_specs = (tile, tile)
    else:
        v1p = jnp.zeros((HW, 128), f32).at[:, :v1.shape[1]].set(v1).astype(bf16)
        v2p = _pad_rows(v2, 0, 128).astype(bf16)
        in_specs = [p_spec, tile] + common + [full((HW, 128)), full((128, HW))]
        args = (p, v_first, mu_pad, par, w2p, a2p, g2p, v1p, v2p)
        out_shape = jax.ShapeDtypeStruct((bsz, t, HW), f32)
        out_specs = tile
    res = pl.pallas_call(
        functools.partial(_rwkv_body, tt=tt, nb=nb, first=first),
        out_shape=out_shape,
        grid=(bsz // nb, t // tt),
        in_specs=in_specs,
        out_specs=out_specs,
        scratch_shapes=[pltpu.VMEM((nb, HW, HW), f32),
                        pltpu.VMEM((nb, 8, 4 * HW), f32)],
        compiler_params=pltpu.CompilerParams(dimension_semantics=("parallel", "arbitrary"),
                                             vmem_limit_bytes=VMEM_LIMIT),
        name="rwkv7_mixer_first" if first else "rwkv7_mixer",
    )(*args)
    return res if first else (res, v_first)


def _mlp_body(x_ref, oa_ref, ob_ref, oc_ref, mod_ref, nw_ref, wo_ref, wu_ref, wd_ref, fw_ref, o_ref, *, last):
    m = mod_ref[0]
    mix = (jnp.dot(oa_ref[0].astype(bf16), wo_ref[0:GDN_W, :], preferred_element_type=f32)
           + jnp.dot(ob_ref[0].astype(bf16), wo_ref[GDN_W:GDN_W + HW, :], preferred_element_type=f32)
           + jnp.dot(oc_ref[0].astype(bf16), wo_ref[GDN_W + HW:, :], preferred_element_type=f32))
    x = x_ref[0] + m[2:3] * mix
    y = x * lax.rsqrt(jnp.mean(x * x, axis=-1, keepdims=True) + NORM_EPS) * nw_ref[...]
    h = (y * (1.0 + m[4:5]) + m[3:4]).astype(bf16)
    acc = jnp.zeros(x.shape, f32)
    nb = 1024
    for j in range(D_FF // nb):
        hid = jnp.dot(h, wu_ref[:, j * nb:(j + 1) * nb], preferred_element_type=f32)
        hid = jnp.square(jnp.maximum(hid, 0.0))
        acc = acc + jnp.dot(hid.astype(bf16), wd_ref[j * nb:(j + 1) * nb, :], preferred_element_type=f32)
    x = x + m[5:6] * acc
    if last:
        x = x * lax.rsqrt(jnp.mean(x * x, axis=-1, keepdims=True) + NORM_EPS) * fw_ref[...]
    o_ref[0] = x


def _out_mlp(x, oa, ob, oc, mod, norm_w, w_out, w_up, w_down, final_w, last, tm):
    bsz, t, _ = x.shape
    tile = lambda w: pl.BlockSpec((1, tm, w), lambda b, i: (b, i, 0))
    const = lambda shape: pl.BlockSpec(shape, lambda b, i: (0, 0), pipeline_mode=pl.Buffered(1))
    return pl.pallas_call(
        functools.partial(_mlp_body, last=last),
        out_shape=jax.ShapeDtypeStruct((bsz, t, D_MODEL), f32),
        grid=(bsz, t // tm),
        in_specs=[tile(D_MODEL), tile(GDN_W), tile(HW), tile(HW),
                  pl.BlockSpec((1, 6, D_MODEL), lambda b, i: (b, 0, 0)),
                  pl.BlockSpec((1, D_MODEL), lambda b, i: (0, 0)),
                  const((D_MODEL, D_MODEL)), const((D_MODEL, D_FF)), const((D_FF, D_MODEL)),
                  pl.BlockSpec((1, D_MODEL), lambda b, i: (0, 0))],
        out_specs=tile(D_MODEL),
        compiler_params=pltpu.CompilerParams(dimension_semantics=("parallel", "parallel"),
                                             vmem_limit_bytes=VMEM_LIMIT),
        name="out_mlp_last" if last else "out_mlp",
    )(x, oa, ob, oc, mod, norm_w.reshape(1, D_MODEL), w_out, w_up, w_down, final_w.reshape(1, D_MODEL))


def _reorder_w_in(w):
    gdn_main = 4 * GDN_W
    ba = 2 * HEADS
    pad = jnp.zeros((w.shape[0], P_COLS - w.shape[1]), w.dtype)
    return jnp.concatenate([w[:, :gdn_main], w[:, gdn_main + ba:], w[:, gdn_main:gdn_main + ba], pad], axis=1)


def kernel(x, c, ada_w, ada_b, norm1_w, norm2_w, w_in, gdn_conv_w, gdn_a_log, gdn_dt_bias, gdn_norm_w, rwkv_mu, rwkv_w0, rwkv_w2, rwkv_a0, rwkv_a2, rwkv_g2, rwkv_k_k, rwkv_k_a, rwkv_r_k, rwkv_ln_w, rwkv_ln_b, rwkv_v0, rwkv_v1, rwkv_v2, w_out, w_up, w_down, final_norm_w):
    bsz, t, _ = x.shape
    depth = ada_w.shape[0]
    tm = min(t, 512)
    tt = min(t, 256)
    nb = 2 if bsz % 2 == 0 else 1
    mod = _modulation(c, ada_w, ada_b).reshape(depth, bsz, 6, D_MODEL)
    cos_t, sin_t = _rope_tables(t)
    v_first = None
    for l in range(depth):
        p = _projection(x, mod[l], norm1_w[l], _reorder_w_in(w_in[l]).astype(bf16), tm)
        o_a = _gdn(p, gdn_conv_w[l], gdn_a_log[l], gdn_dt_bias[l], gdn_norm_w[l], tt, nb)
        o_b = _retention(p, cos_t, sin_t, tt)
        o_c, v_first = _rwkv(p, v_first, rwkv_mu[l], rwkv_w0[l], rwkv_w2[l], rwkv_a0[l], rwkv_a2[l], rwkv_g2[l],
                             rwkv_k_k[l], rwkv_k_a[l], rwkv_r_k[l], rwkv_ln_w[l], rwkv_ln_b[l],
                             None if l == 0 else rwkv_v0[l - 1], None if l == 0 else rwkv_v1[l - 1],
                             None if l == 0 else rwkv_v2[l - 1], tt, nb)
        x = _out_mlp(x, o_a, o_b, o_c, mod[l], norm2_w[l], w_out[l].astype(bf16), w_up[l].astype(bf16),
                     w_down[l].astype(bf16), final_norm_w, l == depth - 1, tm)
    return x
```

```python
import functools
import math

import jax
import jax.numpy as jnp
from jax import lax
from jax.experimental import pallas as pl
from jax.experimental.pallas import tpu as pltpu

f32 = jnp.float32
bf16 = jnp.bfloat16

D_MODEL = 1024
D_FF = 4 * D_MODEL
HEADS = 4
GDN_D = 128
GDN_W = HEADS * GDN_D
GDN_CONV = 4
HD = 64
HW = HEADS * HD
CHUNK = 64
ROPE_BASE = 10000.0
NORM_EPS = 1e-6
L2_EPS = 1e-6
RET_GN_EPS = 1e-6
RWKV_GN_EPS = 64e-5
RWKV_LORA_W = 128

P_GDN = 0
P_RET = 2048
P_RWKV = 3072
P_BA = 3968
P_COLS = 4096

VMEM_LIMIT = 48 * 1024 * 1024


def _iota(shape, dim):
    return lax.broadcasted_iota(jnp.int32, shape, dim)


def _sigmoid(x):
    return jax.nn.sigmoid(x)


def _silu(x):
    return x * jax.nn.sigmoid(x)


def _softplus(x):
    return jnp.maximum(x, 0.0) + jnp.log1p(jnp.exp(-jnp.abs(x)))


_NN = (((1,), (0,)), ((), ()))
_NT = (((1,), (1,)), ((), ()))
_TN = (((0,), (0,)), ((), ()))


def _dg(a, b, dims):
    return lax.dot_general(a, b, dims, preferred_element_type=f32)


def _mm(a, b, dims=_NN):
    return _dg(a.astype(bf16), b.astype(bf16), dims)


def _split3(x):
    hi = x.astype(bf16)
    r1 = x - hi.astype(f32)
    mid = r1.astype(bf16)
    lo = (r1 - mid.astype(f32)).astype(bf16)
    return hi, mid, lo


def _split2(x):
    hi = x.astype(bf16)
    lo = (x - hi.astype(f32)).astype(bf16)
    return hi, lo


def _mm_lexact(a_bf, b, dims=_NN):
    b0, b1, b2 = _split3(b)
    return _dg(a_bf, b0, dims) + _dg(a_bf, b1, dims) + _dg(a_bf, b2, dims)


def _mm_r16(a, b_bf, dims=_NN):
    a0, a1 = _split2(a)
    return _dg(a0, b_bf, dims) + _dg(a1, b_bf, dims)


def _mm3(a, b, dims=_NN):
    a0, a1 = _split2(a)
    b0, b1 = _split2(b)
    return _dg(a0, b0, dims) + (_dg(a0, b1, dims) + _dg(a1, b0, dims))


def _tri_inv_side_many(lows, c):
    n = HEADS * c
    xs = _iota((c, n), 0) ^ (_iota((c, n), 1) % c)
    blk_bf = jnp.where((_iota((n, n), 0) // c) == (_iota((n, n), 1) // c), 1.0, 0.0).astype(bf16)
    to_blk = lambda side: jnp.concatenate([side] * HEADS, axis=0) * blk_bf
    lbs = [low.astype(bf16) for low in lows]
    lblks = [to_blk(lb) for lb in lbs]
    base = jnp.where(xs < 2, 1.0, 0.0).astype(bf16)
    eye = jnp.where(xs == 0, 1.0, 0.0).astype(bf16)
    ts = [lb * base + eye for lb in lbs]
    for s in range(1, 6):
        join = jnp.where((xs >> s) == 1, 1.0, 0.0).astype(bf16)
        zs = [_dg(t, lblk, _NN).astype(bf16) for t, lblk in zip(ts, lblks)]
        ts = [t + _dg(z, to_blk(t), _NN).astype(bf16) * join for t, z in zip(ts, zs)]
    return ts


def _expand_heads(x, head_masks):
    return jnp.concatenate([jnp.where(m, x, 0.0) for m in head_masks], axis=0)


def _collapse_heads(x, c):
    return x[0:c] + x[c:2 * c] + x[2 * c:3 * c] + x[3 * c:4 * c]


def _head_masks():
    lane = _iota((1, HW), 1)
    return [(lane // HD) == h for h in range(HEADS)]


def _block_mask():
    return (_iota((HW, HW), 0) // HD) == (_iota((HW, HW), 1) // HD)


def _mod_body(c_ref, w_ref, b_ref, o_ref):
    cond = _silu(c_ref[...])
    o_ref[0] = _mm3(cond, w_ref[0]) + b_ref[0]


def _modulation(c, ada_w, ada_b):
    depth, _, n = ada_w.shape
    bsz = c.shape[0]
    nb = 512
    return pl.pallas_call(
        _mod_body,
        out_shape=jax.ShapeDtypeStruct((depth, bsz, n), f32),
        grid=(depth, n // nb),
        in_specs=[pl.BlockSpec((bsz, D_MODEL), lambda l, j: (0, 0)),
                  pl.BlockSpec((1, D_MODEL, nb), lambda l, j: (l, 0, j)),
                  pl.BlockSpec((1, 1, nb), lambda l, j: (l, 0, j))],
        out_specs=pl.BlockSpec((1, bsz, nb), lambda l, j: (l, 0, j)),
        compiler_params=pltpu.CompilerParams(dimension_semantics=("parallel", "parallel")),
        name="adaln_mod",
    )(c, ada_w, ada_b.reshape(depth, 1, n))


def _rope_body(cos_ref, sin_ref, *, tt):
    shape = (tt, HW)
    pos = (pl.program_id(0) * tt + _iota(shape, 0)).astype(f32)
    lane = _iota(shape, 1)
    half = HD // 2
    inv_freq = jnp.exp((lane % half).astype(f32) * (-math.log(ROPE_BASE) / half))
    ang = pos * inv_freq
    cos_ref[...] = jnp.cos(ang)
    s = jnp.sin(ang)
    sin_ref[...] = jnp.where((lane % HD) < half, -s, s)


def _rope_tables(t):
    tt = min(t, 512)
    return pl.pallas_call(
        functools.partial(_rope_body, tt=tt),
        out_shape=(jax.ShapeDtypeStruct((t, HW), f32), jax.ShapeDtypeStruct((t, HW), f32)),
        grid=(t // tt,),
        out_specs=(pl.BlockSpec((tt, HW), lambda i: (i, 0)), pl.BlockSpec((tt, HW), lambda i: (i, 0))),
        compiler_params=pltpu.CompilerParams(dimension_semantics=("parallel",)),
        name="rope_tables",
    )()


def _proj_body(x_ref, mod_ref, nw_ref, w_ref, p_ref):
    x = x_ref[0]
    m = mod_ref[0]
    y = x * lax.rsqrt(jnp.mean(x * x, axis=-1, keepdims=True) + NORM_EPS) * nw_ref[...]
    h = y * (1.0 + m[1:2]) + m[0:1]
    p_ref[0] = jnp.dot(h.astype(bf16), w_ref[...], preferred_element_type=f32)


def _projection(x, mod, norm_w, w_in_r, tm):
    bsz, t, _ = x.shape
    return pl.pallas_call(
        _proj_body,
        out_shape=jax.ShapeDtypeStruct((bsz, t, P_COLS), f32),
        grid=(bsz, t // tm),
        in_specs=[pl.BlockSpec((1, tm, D_MODEL), lambda b, i: (b, i, 0)),
                  pl.BlockSpec((1, 6, D_MODEL), lambda b, i: (b, 0, 0)),
                  pl.BlockSpec((1, D_MODEL), lambda b, i: (0, 0)),
                  pl.BlockSpec((D_MODEL, P_COLS), lambda b, i: (0, 0), pipeline_mode=pl.Buffered(1))],
        out_specs=pl.BlockSpec((1, tm, P_COLS), lambda b, i: (b, i, 0)),
        compiler_params=pltpu.CompilerParams(dimension_semantics=("parallel", "parallel"),
                                             vmem_limit_bytes=VMEM_LIMIT),
        name="in_proj",
    )(x, mod, norm_w.reshape(1, D_MODEL), w_in_r)


def _gdn_body(qkvz_ref, ba_ref, cw_ref, gp_ref, nw_ref, o_ref, s_ref, carry_ref, *, tt, nb):
    c = CHUNK
    nc = tt // c

    @pl.when(pl.program_id(1) == 0)
    def _():
        s_ref[...] = jnp.zeros_like(s_ref)
        carry_ref[...] = jnp.zeros_like(carry_ref)

    gp = gp_ref[...]
    ii, jj = _iota((tt, tt), 0), _iota((tt, tt), 1)
    same = (ii // c) == (jj // c)
    same_f = jnp.where(same, 1.0, 0.0)
    cum_mat = jnp.where(same & (jj <= ii), 1.0, 0.0).astype(bf16)
    incl_side = _iota((c, tt), 0) >= (_iota((c, tt), 1) % c)
    strict_side = _iota((c, tt), 0) > (_iota((c, tt), 1) % c)
    sel = jnp.where(_iota((8, 128), 1) == _iota((8, 128), 0) + HEADS, 1.0, 0.0).astype(bf16)
    scale = GDN_D ** -0.5

    lows, attns, rhss, qdecs, kdecs, gls = [], [], [], [], [], []
    for bb in range(nb):
        qkv = []
        for gi in range(3):
            cols = slice(gi * GDN_W, (gi + 1) * GDN_W)
            raw = qkvz_ref[bb, :, cols]
            xw = jnp.concatenate([carry_ref[bb, :, cols], raw], axis=0)
            cw = cw_ref[:, cols]
            y = (cw[3:4] * raw + cw[2:3] * xw[7:7 + tt] + cw[1:2] * xw[6:6 + tt] + cw[0:1] * xw[5:5 + tt])
            y = _silu(y)
            carry_ref[bb, :, cols] = raw[tt - 8:tt]
            segs = [y[:, h * GDN_D:(h + 1) * GDN_D] for h in range(HEADS)]
            if gi < 2:
                segs = [s * lax.rsqrt(jnp.sum(s * s, axis=-1, keepdims=True) + L2_EPS) for s in segs]
            qkv.append(segs)
        qs, ks, vs = qkv

        ba = ba_ref[bb]
        beta_all = _sigmoid(ba)
        g = -jnp.exp(gp[0:1]) * _softplus(ba + gp[1:2])
        gc_all = _mm_lexact(cum_mat, g)
        gl_all = jnp.concatenate(
            [jnp.broadcast_to(gc_all[(ic + 1) * c - 1:(ic + 1) * c], (c, 128)) for ic in range(nc)], axis=0)
        gcrows = _mm_lexact(sel, gc_all, _NT)
        gls.append(gl_all)
        for h in range(HEADS):
            q = qs[h] * scale
            k = ks[h]
            beta = beta_all[:, h:h + 1]
            gc = gc_all[:, HEADS + h:HEADS + h + 1]
            gl = gl_all[:, HEADS + h:HEADS + h + 1]
            gct = jnp.concatenate([jnp.broadcast_to(gc[ic * c:(ic + 1) * c], (c, c)) for ic in range(nc)], axis=1)
            dec = jnp.where(incl_side, jnp.exp(jnp.where(incl_side, gct - gcrows[h:h + 1, :], 0.0)), 0.0)
            kb = k * beta
            eg = jnp.exp(gc)
            k_bf = k.astype(bf16)
            lows.append(-(_collapse_heads(_dg(kb.astype(bf16), k_bf, _NT) * same_f, c)
                          * jnp.where(strict_side, dec, 0.0)))
            attns.append((_collapse_heads(_dg(q.astype(bf16), k_bf, _NT) * same_f, c) * dec).astype(bf16))
            rhss.append(jnp.concatenate([vs[h] * beta, kb * eg], axis=1))
            qdecs.append((q * eg).astype(bf16))
            kdecs.append((k * jnp.exp(gl - gc)).astype(bf16))
    same_bf = same_f.astype(bf16)
    tmats = [jnp.concatenate([ts] * nc, axis=0) * same_bf for ts in _tri_inv_side_many(lows, c)]
    sols = [_dg(tm, rhs.astype(bf16), _NN) for tm, rhs in zip(tmats, rhss)]

    chains = [(bb, h) for bb in range(nb) for h in range(HEADS)]
    rws = lambda ic: slice(ic * c, (ic + 1) * c)
    wqs = [[jnp.concatenate([sols[n][rws(ic), GDN_D:].astype(bf16), qdecs[n][rws(ic)]], axis=0) for ic in range(nc)]
           for n in range(len(chains))]
    cur = [s_ref[bb, h] for bb, h in chains]
    v_new = [[None] * nc for _ in chains]
    cross = [[None] * nc for _ in chains]
    for ic in range(nc):
        ws = [_dg(wqs[n][ic], cur[n].astype(bf16), _NN) for n in range(len(chains))]
        for n, (bb, h) in enumerate(chains):
            v_new[n][ic] = (sols[n][rws(ic), :GDN_D] - ws[n][:c]).astype(bf16)
            cross[n][ic] = ws[n][c:]
            d = jnp.exp(gls[bb][ic * c:ic * c + 1, HEADS + h:HEADS + h + 1])
            cur[n] = cur[n] * d + _dg(kdecs[n][rws(ic)], v_new[n][ic], _TN)
    for n, (bb, h) in enumerate(chains):
        s_ref[bb, h] = cur[n]

    nw = nw_ref[...]
    items = [(n, ic) for ic in range(nc) for n in range(len(chains))]
    outs = [cross[n][ic] + _dg(attns[n][:, ic * c:(ic + 1) * c], v_new[n][ic], _NN) for n, ic in items]
    for (n, ic), o in zip(items, outs):
        bb, h = chains[n]
        o = o * lax.rsqrt(jnp.mean(o * o, axis=-1, keepdims=True) + NORM_EPS) * nw
        z = qkvz_ref[bb, rws(ic), 3 * GDN_W + h * GDN_D:3 * GDN_W + (h + 1) * GDN_D]
        o_ref[bb, rws(ic), h * GDN_D:(h + 1) * GDN_D] = o * _silu(z)


def _gdn(p, conv_w, a_log, dt_bias, norm_w, tt, nb):
    bsz, t, _ = p.shape
    gp = jnp.zeros((8, 128), f32)
    gp = gp.at[0, HEADS:2 * HEADS].set(a_log).at[1, HEADS:2 * HEADS].set(dt_bias)
    return pl.pallas_call(
        functools.partial(_gdn_body, tt=tt, nb=nb),
        out_shape=jax.ShapeDtypeStruct((bsz, t, GDN_W), f32),
        grid=(bsz // nb, t // tt),
        in_specs=[pl.BlockSpec((nb, tt, 4 * GDN_W), lambda b, i: (b, i, P_GDN // (4 * GDN_W))),
                  pl.BlockSpec((nb, tt, 128), lambda b, i: (b, i, P_BA // 128)),
                  pl.BlockSpec((GDN_CONV, 3 * GDN_W), lambda b, i: (0, 0)),
                  pl.BlockSpec((8, 128), lambda b, i: (0, 0)),
                  pl.BlockSpec((1, GDN_D), lambda b, i: (0, 0))],
        out_specs=pl.BlockSpec((nb, tt, GDN_W), lambda b, i: (b, i, 0)),
        scratch_shapes=[pltpu.VMEM((nb, HEADS, GDN_D, GDN_D), f32),
                        pltpu.VMEM((nb, 8, 3 * GDN_W), f32)],
        compiler_params=pltpu.CompilerParams(dimension_semantics=("parallel", "arbitrary"),
                                             vmem_limit_bytes=VMEM_LIMIT),
        name="gdn_mixer",
    )(p, p, conv_w, gp, norm_w.reshape(1, GDN_D))


def _seg_stat(x, blk_bf):
    return _mm_r16(x, blk_bf)


def _ret_body(p_ref, cos_ref, sin_ref, o_ref, r_ref, *, tt):
    @pl.when(pl.program_id(1) == 0)
    def _():
        r_ref[...] = jnp.zeros_like(r_ref)

    lane_t = _iota((tt, HW), 1)
    first_half = (lane_t % HD) < (HD // 2)
    cosf = cos_ref[...]
    sins = sin_ref[...]

    def rotary(x):
        swapped = jnp.where(first_half, pltpu.roll(x, HW - HD // 2, 1), pltpu.roll(x, HD // 2, 1))
        return x * cosf + swapped * sins

    q = rotary(p_ref[0, :, 0:HW])
    k = rotary(p_ref[0, :, HW:2 * HW]) * (HD ** -0.5)
    v = p_ref[0, :, 2 * HW:3 * HW]
    gate = p_ref[0, :, 3 * HW:4 * HW]

    hm = _head_masks()
    blk = _block_mask()
    blk_bf = jnp.where(blk, 1.0, 0.0).astype(bf16)
    log_gamma = [math.log(1.0 - 2.0 ** (-5.0 - h)) for h in range(HEADS)]
    lg = jnp.zeros((1, HW), f32)
    for h in range(HEADS):
        lg = jnp.where(hm[h], log_gamma[h], lg)
    pos = _iota((tt, HW), 0).astype(f32)

    state = r_ref[...]
    o = _mm(q * jnp.exp((pos + 1.0) * lg), state)
    rel = (_iota((tt, tt), 0) - _iota((tt, tt), 1)).astype(f32)
    k_bf = k.astype(bf16)
    for h in range(HEADS):
        dmask = jnp.where(rel >= 0, jnp.exp(jnp.maximum(rel, 0.0) * log_gamma[h]), 0.0)
        scores = _dg(jnp.where(hm[h], q, 0.0).astype(bf16), k_bf, _NT) * dmask
        o = o + _mm(scores, jnp.where(hm[h], v, 0.0))
    upd = _mm(k * jnp.exp((tt - 1.0 - pos) * lg), v, _TN)
    r_ref[...] = state * jnp.exp(float(tt) * lg) + jnp.where(blk, upd, 0.0)

    mu = _seg_stat(o, blk_bf) * (1.0 / HD)
    oc = o - mu
    var = _seg_stat(oc * oc, blk_bf) * (1.0 / HD)
    o_ref[0] = oc * lax.rsqrt(var + RET_GN_EPS) * _silu(gate)


def _retention(p, cos_t, sin_t, tt):
    bsz, t, _ = p.shape
    return pl.pallas_call(
        functools.partial(_ret_body, tt=tt),
        out_shape=jax.ShapeDtypeStruct((bsz, t, HW), f32),
        grid=(bsz, t // tt),
        in_specs=[pl.BlockSpec((1, tt, 4 * HW), lambda b, i: (b, i, P_RET // (4 * HW))),
                  pl.BlockSpec((tt, HW), lambda b, i: (i, 0)),
                  pl.BlockSpec((tt, HW), lambda b, i: (i, 0))],
        out_specs=pl.BlockSpec((1, tt, HW), lambda b, i: (b, i, 0)),
        scratch_shapes=[pltpu.VMEM((HW, HW), f32)],
        compiler_params=pltpu.CompilerParams(dimension_semantics=("parallel", "arbitrary"),
                                             vmem_limit_bytes=VMEM_LIMIT),
        name="retention_mixer",
    )(p, cos_t, sin_t)


_RW_W0, _RW_A0, _RW_KK, _RW_KA, _RW_RK, _RW_LNW, _RW_LNB, _RW_V0 = range(8)


def _rwkv_body(*refs, tt, nb, first):
    if first:
        (p_ref, mu_ref, par_ref, w2_ref, a2_ref, g2_ref, o_ref, vf_out_ref, s_ref, carry_ref) = refs
    else:
        (p_ref, vf_ref, mu_ref, par_ref, w2_ref, a2_ref, g2_ref, v1_ref, v2_ref, o_ref, s_ref, carry_ref) = refs
    c = CHUNK
    nc = tt // c

    @pl.when(pl.program_id(1) == 0)
    def _():
        s_ref[...] = jnp.zeros_like(s_ref)
        carry_ref[...] = jnp.zeros_like(carry_ref)

    blk = _block_mask()
    blk_bf = jnp.where(blk, 1.0, 0.0).astype(bf16)
    par = par_ref[...]
    row = lambda i: par[i:i + 1]
    hm = _head_masks()
    ii, jj = _iota((tt, tt), 0), _iota((tt, tt), 1)
    cum_mat = jnp.where(((ii // c) == (jj // c)) & (jj <= ii), 1.0, 0.0).astype(bf16)
    incl_side = _iota((c, HW), 0) >= (_iota((c, HW), 1) % c)
    strict_side = _iota((c, HW), 0) > (_iota((c, HW), 1) % c)
    ln_w, ln_b = row(_RW_LNW), row(_RW_LNB)

    a_abs, aes, akvs, a_rbs, ylocs = [], [], [], [], []
    per_row = []
    for bb in range(nb):
        raw = p_ref[bb]
        xw = jnp.concatenate([carry_ref[bb], raw], axis=0)
        carry_ref[bb] = raw[tt - 8:tt]
        pm = raw + (xw[7:7 + tt] - raw) * mu_ref[...]
        r = pm[:, 0:HW]
        k = pm[:, HW:2 * HW]
        v = pm[:, 2 * HW:3 * HW]
        lora = pm[:, 3 * HW:3 * HW + RWKV_LORA_W]
        w_log = -_softplus(-(row(_RW_W0) + _mm(jnp.tanh(lora), w2_ref[...]))) - 0.5
        lw = -jnp.exp(w_log)
        a = _sigmoid(row(_RW_A0) + _mm(lora, a2_ref[...]))
        gate = _mm(_sigmoid(lora), g2_ref[...])
        if first:
            vf_out_ref[bb] = v
        else:
            mix = _sigmoid(row(_RW_V0) + _mm(_mm(v, v1_ref[...]), v2_ref[...]))
            v = v + (vf_ref[bb] - v) * mix
        kk = k * row(_RW_KK)
        kk = kk * lax.rsqrt(_seg_stat(kk * kk, blk_bf) + L2_EPS)
        k = k * (1.0 + (a - 1.0) * row(_RW_KA))
        a_ = -kk
        b_ = kk * a
        bonus = _seg_stat(r * k * row(_RW_RK), blk_bf) * v

        gcum = _mm_lexact(cum_mat, lw)
        glast = jnp.concatenate(
            [jnp.broadcast_to(gcum[(ic + 1) * c - 1:(ic + 1) * c], (c, HW)) for ic in range(nc)], axis=0)
        e_neg = jnp.exp(-gcum)
        rt = (r * jnp.exp(gcum)).astype(bf16)
        at = a_ * jnp.exp(gcum - lw)
        kt = k * e_neg
        bt = b_ * e_neg
        e_rem = jnp.exp(glast - gcum)
        kd = (k * e_rem).astype(bf16)
        bd = (b_ * e_rem).astype(bf16)
        per_row.append((rt, v, kd, bd, glast, bonus, gate))

        for ic in range(nc):
            rows = slice(ic * c, (ic + 1) * c)
            ae, ke, be, ve = (_expand_heads(x[rows], hm).astype(bf16) for x in (at, kt, bt, v))
            lhs = jnp.concatenate([at[rows].astype(bf16), rt[rows]], axis=0)
            sk = _dg(lhs, ke, _NT)
            sb = _dg(lhs, be, _NT)
            a_abs.append(jnp.where(strict_side, sb[:c], 0.0))
            a_rbs.append(jnp.where(incl_side, sb[c:], 0.0).astype(bf16))
            kv = _mm(jnp.concatenate([jnp.where(strict_side, sk[:c], 0.0), jnp.where(incl_side, sk[c:], 0.0)],
                                     axis=0), ve)
            aes.append(ae)
            akvs.append(_expand_heads(kv[:c], hm).astype(bf16))
            ylocs.append(kv[c:])
    tsides = _tri_inv_side_many(a_abs, c)
    wps = [_dg(ts, ae, _NN) for ts, ae in zip(tsides, aes)]
    ups = [_dg(ts, akv, _NN) for ts, akv in zip(tsides, akvs)]

    pcs, ncs = [], []
    for bb in range(nb):
        rt, v, kd, bd, glast, bonus, gate = per_row[bb]
        for ic in range(nc):
            rows = slice(ic * c, (ic + 1) * c)
            n = bb * nc + ic
            pcs.append(jnp.where(blk, _mm(wps[n], bd[rows], _TN), 0.0).astype(bf16))
            ncs.append(jnp.where(blk, _mm(jnp.concatenate([v[rows], ups[n]], axis=0),
                                          jnp.concatenate([kd[rows], bd[rows]], axis=0), _TN), 0.0))

    states = [None] * (nb * nc)
    cur = [s_ref[bb] for bb in range(nb)]
    for ic in range(nc):
        for bb in range(nb):
            n = bb * nc + ic
            states[n] = cur[bb].astype(bf16)
            e_last = jnp.exp(per_row[bb][4][ic * c:ic * c + 1, :])
            cur[bb] = cur[bb] * e_last + _dg(states[n], pcs[n], _NN) + ncs[n]
    for bb in range(nb):
        s_ref[bb] = cur[bb]

    rws = lambda n: slice((n % nc) * c, (n % nc + 1) * c)
    rts = [per_row[n // nc][0][rws(n)] for n in range(nb * nc)]
    wss = [_dg(jnp.concatenate([wps[n].astype(bf16), rts[n]], axis=0), states[n], _NT) for n in range(nb * nc)]
    us = [ups[n] + wss[n][:c] for n in range(nb * nc)]
    ys = [wss[n][c:] + ylocs[n] + _mm(a_rbs[n], _expand_heads(us[n], hm)) for n in range(nb * nc)]
    for bb in range(nb):
        rt, v, kd, bd, glast, bonus, gate = per_row[bb]
        y = jnp.concatenate(ys[bb * nc:(bb + 1) * nc], axis=0)
        mu = _seg_stat(y, blk_bf) * (1.0 / HD)
        yc = y - mu
        var = _seg_stat(yc * yc, blk_bf) * (1.0 / HD)
        yn = yc * lax.rsqrt(var + RWKV_GN_EPS) * ln_w + ln_b
        o_ref[bb] = (yn + bonus) * gate


def _pad_rows(w, start, total):
    return jnp.zeros((total, w.shape[1]), f32).at[start:start + w.shape[0]].set(w)


def _rwkv(p, v_first, mu, w0, w2, a0, a2, g2, k_k, k_a, r_k, ln_w, ln_b, v0, v1, v2, tt, nb):
    bsz, t, _ = p.shape
    first = v_first is None
    mu_pad = jnp.zeros((1, 4 * HW), f32).at[0, :mu.shape[0]].set(mu)
    par = jnp.stack([w0, a0, k_k, k_a, r_k.reshape(HW), ln_w, ln_b,
                     jnp.zeros((HW,), f32) if first else v0], axis=0)
    w2p = _pad_rows(w2, 0, RWKV_LORA_W).astype(bf16)
    a2p = _pad_rows(a2, 32, RWKV_LORA_W).astype(bf16)
    g2p = _pad_rows(g2, 64, RWKV_LORA_W).astype(bf16)
    full = lambda shape: pl.BlockSpec(shape, lambda b, i: (0,) * len(shape))
    tile = pl.BlockSpec((nb, tt, HW), lambda b, i: (b, i, 0))
    p_spec = pl.BlockSpec((nb, tt, 4 * HW), lambda b, i: (b, i, P_RWKV // (4 * HW)))
    common = [full((1, 4 * HW)), full((8, HW)), full((RWKV_LORA_W, HW)), full((RWKV_LORA_W, HW)),
              full((RWKV_LORA_W, HW))]
    if first:
        in_specs = [p_spec] + common
        args = (p, mu_pad, par, w2p, a2p, g2p)
        out_shape = (jax.ShapeDtypeStruct((bsz, t, HW), f32), jax.ShapeDtypeStruct((bsz, t, HW), f32))
        out_specs = (tile, tile)
    else:
        v1p = jnp.zeros((HW, 128), f32).at[:, :v1.shape[1]].set(v1).astype(bf16)
        v2p = _pad_rows(v2, 0, 128).astype(bf16)
        in_specs = [p_spec, tile] + common + [full((HW, 128)), full((128, HW))]
        args = (p, v_first, mu_pad, par, w2p, a2p, g2p, v1p, v2p)
        out_shape = jax.ShapeDtypeStruct((bsz, t, HW), f32)
        out_specs = tile
    res = pl.pallas_call(
        functools.partial(_rwkv_body, tt=tt, nb=nb, first=first),
        out_shape=out_shape,
        grid=(bsz // nb, t // tt),
        in_specs=in_specs,
        out_specs=out_specs,
        scratch_shapes=[pltpu.VMEM((nb, HW, HW), f32),
                        pltpu.VMEM((nb, 8, 4 * HW), f32)],
        compiler_params=pltpu.CompilerParams(dimension_semantics=("parallel", "arbitrary"),
                                             vmem_limit_bytes=VMEM_LIMIT),
        name="rwkv7_mixer_first" if first else "rwkv7_mixer",
    )(*args)
    return res if first else (res, v_first)


def _mlp_body(x_ref, oa_ref, ob_ref, oc_ref, mod_ref, nw_ref, wo_ref, wu_ref, wd_ref, fw_ref, o_ref, *, last):
    m = mod_ref[0]
    mix = (jnp.dot(oa_ref[0].astype(bf16), wo_ref[0:GDN_W, :], preferred_element_type=f32)
           + jnp.dot(ob_ref[0].astype(bf16), wo_ref[GDN_W:GDN_W + HW, :], preferred_element_type=f32)
           + jnp.dot(oc_ref[0].astype(bf16), wo_ref[GDN_W + HW:, :], preferred_element_type=f32))
    x = x_ref[0] + m[2:3] * mix
    y = x * lax.rsqrt(jnp.mean(x * x, axis=-1, keepdims=True) + NORM_EPS) * nw_ref[...]
    h = (y * (1.0 + m[4:5]) + m[3:4]).astype(bf16)
    acc = jnp.zeros(x.shape, f32)
    nb = 1024
    for j in range(D_FF // nb):
        hid = jnp.dot(h, wu_ref[:, j * nb:(j + 1) * nb], preferred_element_type=f32)
        hid = jnp.square(jnp.maximum(hid, 0.0))
        acc = acc + jnp.dot(hid.astype(bf16), wd_ref[j * nb:(j + 1) * nb, :], preferred_element_type=f32)
    x = x + m[5:6] * acc
    if last:
        x = x * lax.rsqrt(jnp.mean(x * x, axis=-1, keepdims=True) + NORM_EPS) * fw_ref[...]
    o_ref[0] = x


def _out_mlp(x, oa, ob, oc, mod, norm_w, w_out, w_up, w_down, final_w, last, tm):
    bsz, t, _ = x.shape
    tile = lambda w: pl.BlockSpec((1, tm, w), lambda b, i: (b, i, 0))
    const = lambda shape: pl.BlockSpec(shape, lambda b, i: (0, 0), pipeline_mode=pl.Buffered(1))
    return pl.pallas_call(
        functools.partial(_mlp_body, last=last),
        out_shape=jax.ShapeDtypeStruct((bsz, t, D_MODEL), f32),
        grid=(bsz, t // tm),
        in_specs=[tile(D_MODEL), tile(GDN_W), tile(HW), tile(HW),
                  pl.BlockSpec((1, 6, D_MODEL), lambda b, i: (b, 0, 0)),
                  pl.BlockSpec((1, D_MODEL), lambda b, i: (0, 0)),
                  const((D_MODEL, D_MODEL)), const((D_MODEL, D_FF)), const((D_FF, D_MODEL)),
                  pl.BlockSpec((1, D_MODEL), lambda b, i: (0, 0))],
        out_specs=tile(D_MODEL),
        compiler_params=pltpu.CompilerParams(dimension_semantics=("parallel", "parallel"),
                                             vmem_limit_bytes=VMEM_LIMIT),
        name="out_mlp_last" if last else "out_mlp",
    )(x, oa, ob, oc, mod, norm_w.reshape(1, D_MODEL), w_out, w_up, w_down, final_w.reshape(1, D_MODEL))


def _reorder_w_in(w):
    gdn_main = 4 * GDN_W
    ba = 2 * HEADS
    pad = jnp.zeros((w.shape[0], P_COLS - w.shape[1]), w.dtype)
    return jnp.concatenate([w[:, :gdn_main], w[:, gdn_main + ba:], w[:, gdn_main:gdn_main + ba], pad], axis=1)


def kernel(x, c, ada_w, ada_b, norm1_w, norm2_w, w_in, gdn_conv_w, gdn_a_log, gdn_dt_bias, gdn_norm_w, rwkv_mu, rwkv_w0, rwkv_w2, rwkv_a0, rwkv_a2, rwkv_g2, rwkv_k_k, rwkv_k_a, rwkv_r_k, rwkv_ln_w, rwkv_ln_b, rwkv_v0, rwkv_v1, rwkv_v2, w_out, w_up, w_down, final_norm_w):
    bsz, t, _ = x.shape
    depth = ada_w.shape[0]
    tm = min(t, 512)
    tt = min(t, 256)
    nb = 4 if bsz % 4 == 0 else (2 if bsz % 2 == 0 else 1)
    mod = _modulation(c, ada_w, ada_b).reshape(depth, bsz, 6, D_MODEL)
    cos_t, sin_t = _rope_tables(t)
    v_first = None
    for l in range(depth):
        p = _projection(x, mod[l], norm1_w[l], _reorder_w_in(w_in[l]).astype(bf16), tm)
        o_a = _gdn(p, gdn_conv_w[l], gdn_a_log[l], gdn_dt_bias[l], gdn_norm_w[l], tt, nb)
        o_b = _retention(p, cos_t, sin_t, tt)
        o_c, v_first = _rwkv(p, v_first, rwkv_mu[l], rwkv_w0[l], rwkv_w2[l], rwkv_a0[l], rwkv_a2[l], rwkv_g2[l],
                             rwkv_k_k[l], rwkv_k_a[l], rwkv_r_k[l], rwkv_ln_w[l], rwkv_ln_b[l],
                             None if l == 0 else rwkv_v0[l - 1], None if l == 0 else rwkv_v1[l - 1],
                             None if l == 0 else rwkv_v2[l - 1], tt, nb)
        x = _out_mlp(x, o_a, o_b, o_c, mod[l], norm2_w[l], w_out[l].astype(bf16), w_up[l].astype(bf16),
                     w_down[l].astype(bf16), final_norm_w, l == depth - 1, tm)
    return x
```

```python
import functools
import math

import jax
import jax.numpy as jnp
from jax import lax
from jax.experimental import pallas as pl
from jax.experimental.pallas import tpu as pltpu

f32 = jnp.float32
bf16 = jnp.bfloat16

D_MODEL = 1024
D_FF = 4 * D_MODEL
HEADS = 4
GDN_D = 128
GDN_W = HEADS * GDN_D
GDN_CONV = 4
HD = 64
HW = HEADS * HD
CHUNK = 64
ROPE_BASE = 10000.0
NORM_EPS = 1e-6
L2_EPS = 1e-6
RET_GN_EPS = 1e-6
RWKV_GN_EPS = 64e-5
RWKV_LORA_W = 128

P_GDN = 0
P_RET = 2048
P_RWKV = 3072
P_BA = 3968
P_COLS = 4096

VMEM_LIMIT = 48 * 1024 * 1024


def _iota(shape, dim):
    return lax.broadcasted_iota(jnp.int32, shape, dim)


def _sigmoid(x):
    return jax.nn.sigmoid(x)


def _silu(x):
    return x * jax.nn.sigmoid(x)


def _softplus(x):
    return jnp.maximum(x, 0.0) + jnp.log1p(jnp.exp(-jnp.abs(x)))


_NN = (((1,), (0,)), ((), ()))
_NT = (((1,), (1,)), ((), ()))
_TN = (((0,), (0,)), ((), ()))


def _dg(a, b, dims):
    return lax.dot_general(a, b, dims, preferred_element_type=f32)


def _mm(a, b, dims=_NN):
    return _dg(a.astype(bf16), b.astype(bf16), dims)


def _split3(x):
    hi = x.astype(bf16)
    r1 = x - hi.astype(f32)
    mid = r1.astype(bf16)
    lo = (r1 - mid.astype(f32)).astype(bf16)
    return hi, mid, lo


def _split2(x):
    hi = x.astype(bf16)
    lo = (x - hi.astype(f32)).astype(bf16)
    return hi, lo


def _mm_lexact(a_bf, b, dims=_NN):
    b0, b1, b2 = _split3(b)
    return _dg(a_bf, b0, dims) + _dg(a_bf, b1, dims) + _dg(a_bf, b2, dims)


def _mm_r16(a, b_bf, dims=_NN):
    a0, a1 = _split2(a)
    return _dg(a0, b_bf, dims) + _dg(a1, b_bf, dims)


def _mm3(a, b, dims=_NN):
    a0, a1 = _split2(a)
    b0, b1 = _split2(b)
    return _dg(a0, b0, dims) + (_dg(a0, b1, dims) + _dg(a1, b0, dims))


def _tri_inv_side_many(lows, c):
    n = HEADS * c
    xs = _iota((c, n), 0) ^ (_iota((c, n), 1) % c)
    blk_bf = jnp.where((_iota((n, n), 0) // c) == (_iota((n, n), 1) // c), 1.0, 0.0).astype(bf16)
    to_blk = lambda side: jnp.concatenate([side] * HEADS, axis=0) * blk_bf
    lbs = [low.astype(bf16) for low in lows]
    lblks = [to_blk(lb) for lb in lbs]
    base = jnp.where(xs < 2, 1.0, 0.0).astype(bf16)
    eye = jnp.where(xs == 0, 1.0, 0.0).astype(bf16)
    ts = [lb * base + eye for lb in lbs]
    for s in range(1, 6):
        join = jnp.where((xs >> s) == 1, 1.0, 0.0).astype(bf16)
        zs = [_dg(t, lblk, _NN).astype(bf16) for t, lblk in zip(ts, lblks)]
        ts = [t + _dg(z, to_blk(t), _NN).astype(bf16) * join for t, z in zip(ts, zs)]
    return ts


def _expand_heads(x, head_masks):
    return jnp.concatenate([jnp.where(m, x, 0.0) for m in head_masks], axis=0)


def _collapse_heads(x, c):
    return x[0:c] + x[c:2 * c] + x[2 * c:3 * c] + x[3 * c:4 * c]


def _head_masks():
    lane = _iota((1, HW), 1)
    return [(lane // HD) == h for h in range(HEADS)]


def _block_mask():
    return (_iota((HW, HW), 0) // HD) == (_iota((HW, HW), 1) // HD)


def _mod_body(c_ref, w_ref, b_ref, o_ref):
    cond = _silu(c_ref[...])
    o_ref[0] = _mm3(cond, w_ref[0]) + b_ref[0]


def _modulation(c, ada_w, ada_b):
    depth, _, n = ada_w.shape
    bsz = c.shape[0]
    nb = 512
    return pl.pallas_call(
        _mod_body,
        out_shape=jax.ShapeDtypeStruct((depth, bsz, n), f32),
        grid=(depth, n // nb),
        in_specs=[pl.BlockSpec((bsz, D_MODEL), lambda l, j: (0, 0)),
                  pl.BlockSpec((1, D_MODEL, nb), lambda l, j: (l, 0, j)),
                  pl.BlockSpec((1, 1, nb), lambda l, j: (l, 0, j))],
        out_specs=pl.BlockSpec((1, bsz, nb), lambda l, j: (l, 0, j)),
        compiler_params=pltpu.CompilerParams(dimension_semantics=("parallel", "parallel")),
        name="adaln_mod",
    )(c, ada_w, ada_b.reshape(depth, 1, n))


def _rope_body(cos_ref, sin_ref, *, tt):
    shape = (tt, HW)
    pos = (pl.program_id(0) * tt + _iota(shape, 0)).astype(f32)
    lane = _iota(shape, 1)
    half = HD // 2
    inv_freq = jnp.exp((lane % half).astype(f32) * (-math.log(ROPE_BASE) / half))
    ang = pos * inv_freq
    cos_ref[...] = jnp.cos(ang)
    s = jnp.sin(ang)
    sin_ref[...] = jnp.where((lane % HD) < half, -s, s)


def _rope_tables(t):
    tt = min(t, 512)
    return pl.pallas_call(
        functools.partial(_rope_body, tt=tt),
        out_shape=(jax.ShapeDtypeStruct((t, HW), f32), jax.ShapeDtypeStruct((t, HW), f32)),
        grid=(t // tt,),
        out_specs=(pl.BlockSpec((tt, HW), lambda i: (i, 0)), pl.BlockSpec((tt, HW), lambda i: (i, 0))),
        compiler_params=pltpu.CompilerParams(dimension_semantics=("parallel",)),
        name="rope_tables",
    )()


def _proj_body(x_ref, mod_ref, nw_ref, w_ref, cw_ref, p_ref, carry_ref):
    tm = x_ref.shape[1]

    @pl.when(pl.program_id(1) == 0)
    def _():
        carry_ref[...] = jnp.zeros_like(carry_ref)

    x = x_ref[0]
    m = mod_ref[0]
    y = x * lax.rsqrt(jnp.mean(x * x, axis=-1, keepdims=True) + NORM_EPS) * nw_ref[...]
    h = (y * (1.0 + m[1:2]) + m[0:1]).astype(bf16)
    p_ref[0] = jnp.dot(h, w_ref[...], preferred_element_type=f32)

    for gi in range(3):
        cols = slice(gi * GDN_W, (gi + 1) * GDN_W)
        raw = p_ref[0, :, cols]
        xw = jnp.concatenate([carry_ref[:, cols], raw], axis=0)
        cw = cw_ref[:, cols]
        z = _silu(cw[3:4] * raw + cw[2:3] * xw[7:7 + tm] + cw[1:2] * xw[6:6 + tm] + cw[0:1] * xw[5:5 + tm])
        carry_ref[:, cols] = raw[tm - 8:tm]
        if gi < 2:
            for hd in range(HEADS):
                seg = z[:, hd * GDN_D:(hd + 1) * GDN_D]
                p_ref[0, :, gi * GDN_W + hd * GDN_D:gi * GDN_W + (hd + 1) * GDN_D] = seg * lax.rsqrt(
                    jnp.sum(seg * seg, axis=-1, keepdims=True) + L2_EPS)
        else:
            p_ref[0, :, cols] = z


def _projection(x, mod, norm_w, w_in_r, conv_w, tm):
    bsz, t, _ = x.shape
    return pl.pallas_call(
        _proj_body,
        out_shape=jax.ShapeDtypeStruct((bsz, t, P_COLS), f32),
        grid=(bsz, t // tm),
        in_specs=[pl.BlockSpec((1, tm, D_MODEL), lambda b, i: (b, i, 0)),
                  pl.BlockSpec((1, 6, D_MODEL), lambda b, i: (b, 0, 0)),
                  pl.BlockSpec((1, D_MODEL), lambda b, i: (0, 0)),
                  pl.BlockSpec((D_MODEL, P_COLS), lambda b, i: (0, 0), pipeline_mode=pl.Buffered(1)),
                  pl.BlockSpec((GDN_CONV, 3 * GDN_W), lambda b, i: (0, 0))],
        out_specs=pl.BlockSpec((1, tm, P_COLS), lambda b, i: (b, i, 0)),
        scratch_shapes=[pltpu.VMEM((8, 3 * GDN_W), f32)],
        compiler_params=pltpu.CompilerParams(dimension_semantics=("parallel", "arbitrary"),
                                             vmem_limit_bytes=VMEM_LIMIT),
        name="in_proj",
    )(x, mod, norm_w.reshape(1, D_MODEL), w_in_r, conv_w)


def _gdn_body(qkvz_ref, ba_ref, gp_ref, nw_ref, o_ref, s_ref, *, tt, nb):
    c = CHUNK
    nc = tt // c

    @pl.when(pl.program_id(1) == 0)
    def _():
        s_ref[...] = jnp.zeros_like(s_ref)

    gp = gp_ref[...]
    ii, jj = _iota((tt, tt), 0), _iota((tt, tt), 1)
    same = (ii // c) == (jj // c)
    same_f = jnp.where(same, 1.0, 0.0)
    cum_mat = jnp.where(same & (jj <= ii), 1.0, 0.0).astype(bf16)
    incl_side = _iota((c, tt), 0) >= (_iota((c, tt), 1) % c)
    strict_side = _iota((c, tt), 0) > (_iota((c, tt), 1) % c)
    sel = jnp.where(_iota((8, 128), 1) == _iota((8, 128), 0) + HEADS, 1.0, 0.0).astype(bf16)
    scale = GDN_D ** -0.5

    lows, attns, rhss, qdecs, kdecs, gls = [], [], [], [], [], []
    for bb in range(nb):
        qs, ks, vs = ([qkvz_ref[bb, :, gi * GDN_W + h * GDN_D:gi * GDN_W + (h + 1) * GDN_D] for h in range(HEADS)]
                      for gi in range(3))

        ba = ba_ref[bb]
        beta_all = _sigmoid(ba)
        g = -jnp.exp(gp[0:1]) * _softplus(ba + gp[1:2])
        gc_all = _mm_lexact(cum_mat, g)
        gl_all = jnp.concatenate(
            [jnp.broadcast_to(gc_all[(ic + 1) * c - 1:(ic + 1) * c], (c, 128)) for ic in range(nc)], axis=0)
        gcrows = _mm_lexact(sel, gc_all, _NT)
        gls.append(gl_all)
        for h in range(HEADS):
            q = qs[h] * scale
            k = ks[h]
            beta = beta_all[:, h:h + 1]
            gc = gc_all[:, HEADS + h:HEADS + h + 1]
            gl = gl_all[:, HEADS + h:HEADS + h + 1]
            gct = jnp.concatenate([jnp.broadcast_to(gc[ic * c:(ic + 1) * c], (c, c)) for ic in range(nc)], axis=1)
            dec = jnp.where(incl_side, jnp.exp(jnp.where(incl_side, gct - gcrows[h:h + 1, :], 0.0)), 0.0)
            kb = k * beta
            eg = jnp.exp(gc)
            k_bf = k.astype(bf16)
            lows.append(-(_collapse_heads(_dg(kb.astype(bf16), k_bf, _NT) * same_f, c)
                          * jnp.where(strict_side, dec, 0.0)))
            attns.append((_collapse_heads(_dg(q.astype(bf16), k_bf, _NT) * same_f, c) * dec).astype(bf16))
            rhss.append(jnp.concatenate([vs[h] * beta, kb * eg], axis=1))
            qdecs.append((q * eg).astype(bf16))
            kdecs.append((k * jnp.exp(gl - gc)).astype(bf16))
    same_bf = same_f.astype(bf16)
    tmats = [jnp.concatenate([ts] * nc, axis=0) * same_bf for ts in _tri_inv_side_many(lows, c)]
    sols = [_dg(tm, rhs.astype(bf16), _NN) for tm, rhs in zip(tmats, rhss)]

    chains = [(bb, h) for bb in range(nb) for h in range(HEADS)]
    rws = lambda ic: slice(ic * c, (ic + 1) * c)
    wqs = [[jnp.concatenate([sols[n][rws(ic), GDN_D:].astype(bf16), qdecs[n][rws(ic)]], axis=0) for ic in range(nc)]
           for n in range(len(chains))]
    cur = [s_ref[bb, h] for bb, h in chains]
    v_new = [[None] * nc for _ in chains]
    cross = [[None] * nc for _ in chains]
    for ic in range(nc):
        ws = [_dg(wqs[n][ic], cur[n].astype(bf16), _NN) for n in range(len(chains))]
        for n, (bb, h) in enumerate(chains):
            v_new[n][ic] = (sols[n][rws(ic), :GDN_D] - ws[n][:c]).astype(bf16)
            cross[n][ic] = ws[n][c:]
            d = jnp.exp(gls[bb][ic * c:ic * c + 1, HEADS + h:HEADS + h + 1])
            cur[n] = cur[n] * d + _dg(kdecs[n][rws(ic)], v_new[n][ic], _TN)
    for n, (bb, h) in enumerate(chains):
        s_ref[bb, h] = cur[n]

    nw = nw_ref[...]
    items = [(n, ic) for ic in range(nc) for n in range(len(chains))]
    outs = [cross[n][ic] + _dg(attns[n][:, ic * c:(ic + 1) * c], v_new[n][ic], _NN) for n, ic in items]
    for (n, ic), o in zip(items, outs):
        bb, h = chains[n]
        o = o * lax.rsqrt(jnp.mean(o * o, axis=-1, keepdims=True) + NORM_EPS) * nw
        z = qkvz_ref[bb, rws(ic), 3 * GDN_W + h * GDN_D:3 * GDN_W + (h + 1) * GDN_D]
        o_ref[bb, rws(ic), h * GDN_D:(h + 1) * GDN_D] = o * _silu(z)


def _gdn(p, a_log, dt_bias, norm_w, tt, nb):
    bsz, t, _ = p.shape
    gp = jnp.zeros((8, 128), f32)
    gp = gp.at[0, HEADS:2 * HEADS].set(a_log).at[1, HEADS:2 * HEADS].set(dt_bias)
    return pl.pallas_call(
        functools.partial(_gdn_body, tt=tt, nb=nb),
        out_shape=jax.ShapeDtypeStruct((bsz, t, GDN_W), f32),
        grid=(bsz // nb, t // tt),
        in_specs=[pl.BlockSpec((nb, tt, 4 * GDN_W), lambda b, i: (b, i, P_GDN // (4 * GDN_W))),
                  pl.BlockSpec((nb, tt, 128), lambda b, i: (b, i, P_BA // 128)),
                  pl.BlockSpec((8, 128), lambda b, i: (0, 0)),
                  pl.BlockSpec((1, GDN_D), lambda b, i: (0, 0))],
        out_specs=pl.BlockSpec((nb, tt, GDN_W), lambda b, i: (b, i, 0)),
        scratch_shapes=[pltpu.VMEM((nb, HEADS, GDN_D, GDN_D), f32)],
        compiler_params=pltpu.CompilerParams(dimension_semantics=("parallel", "arbitrary"),
                                             vmem_limit_bytes=VMEM_LIMIT),
        name="gdn_mixer",
    )(p, p, gp, norm_w.reshape(1, GDN_D))


def _seg_stat(x, blk_bf):
    return _mm_r16(x, blk_bf)


def _ret_body(p_ref, cos_ref, sin_ref, o_ref, r_ref, *, tt, nb):
    c = CHUNK
    nc = tt // c

    @pl.when(pl.program_id(1) == 0)
    def _():
        r_ref[...] = jnp.zeros_like(r_ref)

    lane_t = _iota((tt, HW), 1)
    first_half = (lane_t % HD) < (HD // 2)
    cosf = cos_ref[...]
    sins = sin_ref[...]

    def rotary(x):
        swapped = jnp.where(first_half, pltpu.roll(x, HW - HD // 2, 1), pltpu.roll(x, HD // 2, 1))
        return x * cosf + swapped * sins

    hm = _head_masks()
    blk = _block_mask()
    blk_bf = jnp.where(blk, 1.0, 0.0).astype(bf16)
    lg = jnp.zeros((1, HW), f32)
    for h in range(HEADS):
        lg = jnp.where(hm[h], math.log(1.0 - 2.0 ** (-5.0 - h)), lg)
    pos = _iota((c, HW), 0).astype(f32)
    q_decay = jnp.exp((pos + 1.0) * lg)
    k_decay = jnp.exp((c - 1.0 - pos) * lg)
    chunk_decay = jnp.exp(float(c) * lg)
    rel = (_iota((c, HW), 0) - (_iota((c, HW), 1) % c)).astype(f32)
    dmask = jnp.where(rel >= 0, jnp.exp(jnp.maximum(rel, 0.0) * lg), 0.0)

    items = [(bb, ic) for bb in range(nb) for ic in range(nc)]
    rws = lambda ic: slice(ic * c, (ic + 1) * c)
    qs = [rotary(p_ref[bb, :, 0:HW]) for bb in range(nb)]
    ks = [rotary(p_ref[bb, :, HW:2 * HW]) * (HD ** -0.5) for bb in range(nb)]
    kes = [_expand_heads(ks[bb][rws(ic)], hm).astype(bf16) for bb, ic in items]
    ves = [_expand_heads(p_ref[bb, rws(ic), 2 * HW:3 * HW], hm).astype(bf16) for bb, ic in items]
    scores = [(_dg(qs[bb][rws(ic)].astype(bf16), ke, _NT) * dmask).astype(bf16) for (bb, ic), ke in zip(items, kes)]
    inner = [_dg(s, ve, _NN) for s, ve in zip(scores, ves)]
    upds = [jnp.where(blk, _mm(ks[bb][rws(ic)] * k_decay, p_ref[bb, rws(ic), 2 * HW:3 * HW], _TN), 0.0)
            for bb, ic in items]

    states = []
    for bb in range(nb):
        cur = r_ref[bb]
        for ic in range(nc):
            states.append(cur.astype(bf16))
            cur = cur * chunk_decay + upds[bb * nc + ic]
        r_ref[bb] = cur
    cross = [_dg((qs[bb][rws(ic)] * q_decay).astype(bf16), st, _NN) for (bb, ic), st in zip(items, states)]

    for bb in range(nb):
        o = jnp.concatenate([inner[bb * nc + ic] + cross[bb * nc + ic] for ic in range(nc)], axis=0)
        mu = _seg_stat(o, blk_bf) * (1.0 / HD)
        oc = o - mu
        var = _seg_stat(oc * oc, blk_bf) * (1.0 / HD)
        o_ref[bb] = oc * lax.rsqrt(var + RET_GN_EPS) * _silu(p_ref[bb, :, 3 * HW:4 * HW])


def _retention(p, cos_t, sin_t, tt, nb):
    bsz, t, _ = p.shape
    return pl.pallas_call(
        functools.partial(_ret_body, tt=tt, nb=nb),
        out_shape=jax.ShapeDtypeStruct((bsz, t, HW), f32),
        grid=(bsz // nb, t // tt),
        in_specs=[pl.BlockSpec((nb, tt, 4 * HW), lambda b, i: (b, i, P_RET // (4 * HW))),
                  pl.BlockSpec((tt, HW), lambda b, i: (i, 0)),
                  pl.BlockSpec((tt, HW), lambda b, i: (i, 0))],
        out_specs=pl.BlockSpec((nb, tt, HW), lambda b, i: (b, i, 0)),
        scratch_shapes=[pltpu.VMEM((nb, HW, HW), f32)],
        compiler_params=pltpu.CompilerParams(dimension_semantics=("parallel", "arbitrary"),
                                             vmem_limit_bytes=VMEM_LIMIT),
        name="retention_mixer",
    )(p, cos_t, sin_t)


_RW_W0, _RW_A0, _RW_KK, _RW_KA, _RW_RK, _RW_LNW, _RW_LNB, _RW_V0 = range(8)


def _rwkv_body(*refs, tt, nb, first):
    if first:
        (p_ref, mu_ref, par_ref, w2_ref, a2_ref, g2_ref, o_ref, vf_out_ref, s_ref, carry_ref) = refs
    else:
        (p_ref, vf_ref, mu_ref, par_ref, w2_ref, a2_ref, g2_ref, v1_ref, v2_ref, o_ref, s_ref, carry_ref) = refs
    c = CHUNK
    nc = tt // c

    @pl.when(pl.program_id(1) == 0)
    def _():
        s_ref[...] = jnp.zeros_like(s_ref)
        carry_ref[...] = jnp.zeros_like(carry_ref)

    blk = _block_mask()
    blk_bf = jnp.where(blk, 1.0, 0.0).astype(bf16)
    par = par_ref[...]
    row = lambda i: par[i:i + 1]
    hm = _head_masks()
    ii, jj = _iota((tt, tt), 0), _iota((tt, tt), 1)
    cum_mat = jnp.where(((ii // c) == (jj // c)) & (jj <= ii), 1.0, 0.0).astype(bf16)
    incl_side = _iota((c, HW), 0) >= (_iota((c, HW), 1) % c)
    strict_side = _iota((c, HW), 0) > (_iota((c, HW), 1) % c)
    ln_w, ln_b = row(_RW_LNW), row(_RW_LNB)

    a_abs, aes, akvs, a_rbs, ylocs = [], [], [], [], []
    per_row = []
    for bb in range(nb):
        raw = p_ref[bb]
        xw = jnp.concatenate([carry_ref[bb], raw], axis=0)
        carry_ref[bb] = raw[tt - 8:tt]
        pm = raw + (xw[7:7 + tt] - raw) * mu_ref[...]
        r = pm[:, 0:HW]
        k = pm[:, HW:2 * HW]
        v = pm[:, 2 * HW:3 * HW]
        lora = pm[:, 3 * HW:3 * HW + RWKV_LORA_W]
        w_log = -_softplus(-(row(_RW_W0) + _mm(jnp.tanh(lora), w2_ref[...]))) - 0.5
        lw = -jnp.exp(w_log)
        a = _sigmoid(row(_RW_A0) + _mm(lora, a2_ref[...]))
        gate = _mm(_sigmoid(lora), g2_ref[...])
        if first:
            vf_out_ref[bb] = v
        else:
            mix = _sigmoid(row(_RW_V0) + _mm(_mm(v, v1_ref[...]), v2_ref[...]))
            v = v + (vf_ref[bb] - v) * mix
        kk = k * row(_RW_KK)
        kk = kk * lax.rsqrt(_seg_stat(kk * kk, blk_bf) + L2_EPS)
        k = k * (1.0 + (a - 1.0) * row(_RW_KA))
        a_ = -kk
        b_ = kk * a
        bonus = _seg_stat(r * k * row(_RW_RK), blk_bf) * v

        gcum = _mm_lexact(cum_mat, lw)
        glast = jnp.concatenate(
            [jnp.broadcast_to(gcum[(ic + 1) * c - 1:(ic + 1) * c], (c, HW)) for ic in range(nc)], axis=0)
        e_neg = jnp.exp(-gcum)
        rt = (r * jnp.exp(gcum)).astype(bf16)
        at = a_ * jnp.exp(gcum - lw)
        kt = k * e_neg
        bt = b_ * e_neg
        e_rem = jnp.exp(glast - gcum)
        kd = (k * e_rem).astype(bf16)
        bd = (b_ * e_rem).astype(bf16)
        per_row.append((rt, v, kd, bd, glast, bonus, gate))

        for ic in range(nc):
            rows = slice(ic * c, (ic + 1) * c)
            ae, ke, be, ve = (_expand_heads(x[rows], hm).astype(bf16) for x in (at, kt, bt, v))
            lhs = jnp.concatenate([at[rows].astype(bf16), rt[rows]], axis=0)
            sk = _dg(lhs, ke, _NT)
            sb = _dg(lhs, be, _NT)
            a_abs.append(jnp.where(strict_side, sb[:c], 0.0))
            a_rbs.append(jnp.where(incl_side, sb[c:], 0.0).astype(bf16))
            kv = _mm(jnp.concatenate([jnp.where(strict_side, sk[:c], 0.0), jnp.where(incl_side, sk[c:], 0.0)],
                                     axis=0), ve)
            aes.append(ae)
            akvs.append(_expand_heads(kv[:c], hm).astype(bf16))
            ylocs.append(kv[c:])
    tsides = _tri_inv_side_many(a_abs, c)
    wps = [_dg(ts, ae, _NN) for ts, ae in zip(tsides, aes)]
    ups = [_dg(ts, akv, _NN) for ts, akv in zip(tsides, akvs)]

    pcs, ncs = [], []
    for bb in range(nb):
        rt, v, kd, bd, glast, bonus, gate = per_row[bb]
        for ic in range(nc):
            rows = slice(ic * c, (ic + 1) * c)
            n = bb * nc + ic
            pcs.append(jnp.where(blk, _mm(wps[n], bd[rows], _TN), 0.0).astype(bf16))
            ncs.append(jnp.where(blk, _mm(jnp.concatenate([v[rows], ups[n]], axis=0),
                                          jnp.concatenate([kd[rows], bd[rows]], axis=0), _TN), 0.0))

    states = [None] * (nb * nc)
    cur = [s_ref[bb] for bb in range(nb)]
    for ic in range(nc):
        for bb in range(nb):
            n = bb * nc + ic
            states[n] = cur[bb].astype(bf16)
            e_last = jnp.exp(per_row[bb][4][ic * c:ic * c + 1, :])
            cur[bb] = cur[bb] * e_last + _dg(states[n], pcs[n], _NN) + ncs[n]
    for bb in range(nb):
        s_ref[bb] = cur[bb]

    rws = lambda n: slice((n % nc) * c, (n % nc + 1) * c)
    rts = [per_row[n // nc][0][rws(n)] for n in range(nb * nc)]
    wss = [_dg(jnp.concatenate([wps[n].astype(bf16), rts[n]], axis=0), states[n], _NT) for n in range(nb * nc)]
    us = [ups[n] + wss[n][:c] for n in range(nb * nc)]
    ys = [wss[n][c:] + ylocs[n] + _mm(a_rbs[n], _expand_heads(us[n], hm)) for n in range(nb * nc)]
    for bb in range(nb):
        rt, v, kd, bd, glast, bonus, gate = per_row[bb]
        y = jnp.concatenate(ys[bb * nc:(bb + 1) * nc], axis=0)
        mu = _seg_stat(y, blk_bf) * (1.0 / HD)
        yc = y - mu
        var = _seg_stat(yc * yc, blk_bf) * (1.0 / HD)
        yn = yc * lax.rsqrt(var + RWKV_GN_EPS) * ln_w + ln_b
        o_ref[bb] = (yn + bonus) * gate


def _pad_rows(w, start, total):
    return jnp.zeros((total, w.shape[1]), f32).at[start:start + w.shape[0]].set(w)


def _rwkv(p, v_first, mu, w0, w2, a0, a2, g2, k_k, k_a, r_k, ln_w, ln_b, v0, v1, v2, tt, nb):
    bsz, t, _ = p.shape
    first = v_first is None
    mu_pad = jnp.zeros((1, 4 * HW), f32).at[0, :mu.shape[0]].set(mu)
    par = jnp.stack([w0, a0, k_k, k_a, r_k.reshape(HW), ln_w, ln_b,
                     jnp.zeros((HW,), f32) if first else v0], axis=0)
    w2p = _pad_rows(w2, 0, RWKV_LORA_W).astype(bf16)
    a2p = _pad_rows(a2, 32, RWKV_LORA_W).astype(bf16)
    g2p = _pad_rows(g2, 64, RWKV_LORA_W).astype(bf16)
    full = lambda shape: pl.BlockSpec(shape, lambda b, i: (0,) * len(shape))
    tile = pl.BlockSpec((nb, tt, HW), lambda b, i: (b, i, 0))
    p_spec = pl.BlockSpec((nb, tt, 4 * HW), lambda b, i: (b, i, P_RWKV // (4 * HW)))
    common = [full((1, 4 * HW)), full((8, HW)), full((RWKV_LORA_W, HW)), full((RWKV_LORA_W, HW)),
              full((RWKV_LORA_W, HW))]
    if first:
        in_specs = [p_spec] + common
        args = (p, mu_pad, par, w2p, a2p, g2p)
        out_shape = (jax.ShapeDtypeStruct((bsz, t, HW), f32), jax.ShapeDtypeStruct((bsz, t, HW), f32))
        out_specs = (tile, tile)
    else:
        v1p = jnp.zeros((HW, 128), f32).at[:, :v1.shape[1]].set(v1).astype(bf16)
        v2p = _pad_rows(v2, 0, 128).astype(bf16)
        in_specs = [p_spec, tile] + common + [full((HW, 128)), full((128, HW))]
        args = (p, v_first, mu_pad, par, w2p, a2p, g2p, v1p, v2p)
        out_shape = jax.ShapeDtypeStruct((bsz, t, HW), f32)
        out_specs = tile
    res = pl.pallas_call(
        functools.partial(_rwkv_body, tt=tt, nb=nb, first=first),
        out_shape=out_shape,
        grid=(bsz // nb, t // tt),
        in_specs=in_specs,
        out_specs=out_specs,
        scratch_shapes=[pltpu.VMEM((nb, HW, HW), f32),
                        pltpu.VMEM((nb, 8, 4 * HW), f32)],
        compiler_params=pltpu.CompilerParams(dimension_semantics=("parallel", "arbitrary"),
                                             vmem_limit_bytes=VMEM_LIMIT),
        name="rwkv7_mixer_first" if first else "rwkv7_mixer",
    )(*args)
    return res if first else (res, v_first)


def _mlp_body(x_ref, oa_ref, ob_ref, oc_ref, mod_ref, nw_ref, wo_ref, wu_ref, wd_ref, fw_ref, o_ref, *, last):
    m = mod_ref[0]
    mix = (jnp.dot(oa_ref[0].astype(bf16), wo_ref[0:GDN_W, :], preferred_element_type=f32)
           + jnp.dot(ob_ref[0].astype(bf16), wo_ref[GDN_W:GDN_W + HW, :], preferred_element_type=f32)
           + jnp.dot(oc_ref[0].astype(bf16), wo_ref[GDN_W + HW:, :], preferred_element_type=f32))
    x = x_ref[0] + m[2:3] * mix
    y = x * lax.rsqrt(jnp.mean(x * x, axis=-1, keepdims=True) + NORM_EPS) * nw_ref[...]
    h = (y * (1.0 + m[4:5]) + m[3:4]).astype(bf16)
    acc = jnp.zeros(x.shape, f32)
    nb = 1024
    for j in range(D_FF // nb):
        hid = jnp.dot(h, wu_ref[:, j * nb:(j + 1) * nb], preferred_element_type=f32)
        hid = jnp.square(jnp.maximum(hid, 0.0))
        acc = acc + jnp.dot(hid.astype(bf16), wd_ref[j * nb:(j + 1) * nb, :], preferred_element_type=f32)
    x = x + m[5:6] * acc
    if last:
        x = x * lax.rsqrt(jnp.mean(x * x, axis=-1, keepdims=True) + NORM_EPS) * fw_ref[...]
    o_ref[0] = x


def _out_mlp(x, oa, ob, oc, mod, norm_w, w_out, w_up, w_down, final_w, last, tm):
    bsz, t, _ = x.shape
    tile = lambda w: pl.BlockSpec((1, tm, w), lambda b, i: (b, i, 0))
    const = lambda shape: pl.BlockSpec(shape, lambda b, i: (0, 0), pipeline_mode=pl.Buffered(1))
    return pl.pallas_call(
        functools.partial(_mlp_body, last=last),
        out_shape=jax.ShapeDtypeStruct((bsz, t, D_MODEL), f32),
        grid=(bsz, t // tm),
        in_specs=[tile(D_MODEL), tile(GDN_W), tile(HW), tile(HW),
                  pl.BlockSpec((1, 6, D_MODEL), lambda b, i: (b, 0, 0)),
                  pl.BlockSpec((1, D_MODEL), lambda b, i: (0, 0)),
                  const((D_MODEL, D_MODEL)), const((D_MODEL, D_FF)), const((D_FF, D_MODEL)),
                  pl.BlockSpec((1, D_MODEL), lambda b, i: (0, 0))],
        out_specs=tile(D_MODEL),
        compiler_params=pltpu.CompilerParams(dimension_semantics=("parallel", "parallel"),
                                             vmem_limit_bytes=VMEM_LIMIT),
        name="out_mlp_last" if last else "out_mlp",
    )(x, oa, ob, oc, mod, norm_w.reshape(1, D_MODEL), w_out, w_up, w_down, final_w.reshape(1, D_MODEL))


def _reorder_w_in(w):
    gdn_main = 4 * GDN_W
    ba = 2 * HEADS
    pad = jnp.zeros((w.shape[0], P_COLS - w.shape[1]), w.dtype)
    return jnp.concatenate([w[:, :gdn_main], w[:, gdn_main + ba:], w[:, gdn_main:gdn_main + ba], pad], axis=1)


def kernel(x, c, ada_w, ada_b, norm1_w, norm2_w, w_in, gdn_conv_w, gdn_a_log, gdn_dt_bias, gdn_norm_w, rwkv_mu, rwkv_w0, rwkv_w2, rwkv_a0, rwkv_a2, rwkv_g2, rwkv_k_k, rwkv_k_a, rwkv_r_k, rwkv_ln_w, rwkv_ln_b, rwkv_v0, rwkv_v1, rwkv_v2, w_out, w_up, w_down, final_norm_w):
    bsz, t, _ = x.shape
    depth = ada_w.shape[0]
    tm = min(t, 512)
    tt = min(t, 256)
    nb = 4 if bsz % 4 == 0 else (2 if bsz % 2 == 0 else 1)
    mod = _modulation(c, ada_w, ada_b).reshape(depth, bsz, 6, D_MODEL)
    cos_t, sin_t = _rope_tables(t)
    v_first = None
    for l in range(depth):
        p = _projection(x, mod[l], norm1_w[l], _reorder_w_in(w_in[l]).astype(bf16), gdn_conv_w[l], tm)
        o_a = _gdn(p, gdn_a_log[l], gdn_dt_bias[l], gdn_norm_w[l], tt, nb)
        o_b = _retention(p, cos_t, sin_t, tt, nb)
        o_c, v_first = _rwkv(p, v_first, rwkv_mu[l], rwkv_w0[l], rwkv_w2[l], rwkv_a0[l], rwkv_a2[l], rwkv_g2[l],
                             rwkv_k_k[l], rwkv_k_a[l], rwkv_r_k[l], rwkv_ln_w[l], rwkv_ln_b[l],
                             None if l == 0 else rwkv_v0[l - 1], None if l == 0 else rwkv_v1[l - 1],
                             None if l == 0 else rwkv_v2[l - 1], tt, nb)
        x = _out_mlp(x, o_a, o_b, o_c, mod[l], norm2_w[l], w_out[l].astype(bf16), w_up[l].astype(bf16),
                     w_down[l].astype(bf16), final_norm_w, l == depth - 1, tm)
    return x
```

```python
import functools
import math

import jax
import jax.numpy as jnp
from jax import lax
from jax.experimental import pallas as pl
from jax.experimental.pallas import tpu as pltpu

f32 = jnp.float32
bf16 = jnp.bfloat16

D_MODEL = 1024
D_FF = 4 * D_MODEL
HEADS = 4
GDN_D = 128
GDN_W = HEADS * GDN_D
GDN_CONV = 4
HD = 64
HW = HEADS * HD
CHUNK = 64
ROPE_BASE = 10000.0
NORM_EPS = 1e-6
L2_EPS = 1e-6
RET_GN_EPS = 1e-6
RWKV_GN_EPS = 64e-5
RWKV_LORA_W = 128

P_GDN = 0
P_RET = 2048
P_RWKV = 3072
P_BA = 3968
P_COLS = 4096

VMEM_LIMIT = 48 * 1024 * 1024


def _iota(shape, dim):
    return lax.broadcasted_iota(jnp.int32, shape, dim)


def _sigmoid(x):
    return jax.nn.sigmoid(x)


def _silu(x):
    return x * jax.nn.sigmoid(x)


def _softplus(x):
    return jnp.maximum(x, 0.0) + jnp.log1p(jnp.exp(-jnp.abs(x)))


_NN = (((1,), (0,)), ((), ()))
_NT = (((1,), (1,)), ((), ()))
_TN = (((0,), (0,)), ((), ()))


def _dg(a, b, dims):
    return lax.dot_general(a, b, dims, preferred_element_type=f32)


def _mm(a, b, dims=_NN):
    return _dg(a.astype(bf16), b.astype(bf16), dims)


def _split3(x):
    hi = x.astype(bf16)
    r1 = x - hi.astype(f32)
    mid = r1.astype(bf16)
    lo = (r1 - mid.astype(f32)).astype(bf16)
    return hi, mid, lo


def _split2(x):
    hi = x.astype(bf16)
    lo = (x - hi.astype(f32)).astype(bf16)
    return hi, lo


def _mm_lexact(a_bf, b, dims=_NN):
    b0, b1, b2 = _split3(b)
    return _dg(a_bf, b0, dims) + _dg(a_bf, b1, dims) + _dg(a_bf, b2, dims)


def _mm_r16(a, b_bf, dims=_NN):
    a0, a1 = _split2(a)
    return _dg(a0, b_bf, dims) + _dg(a1, b_bf, dims)


def _mm3(a, b, dims=_NN):
    a0, a1 = _split2(a)
    b0, b1 = _split2(b)
    return _dg(a0, b0, dims) + (_dg(a0, b1, dims) + _dg(a1, b0, dims))


def _tri_inv_side_many(lows, c):
    n = HEADS * c
    xs = _iota((c, n), 0) ^ (_iota((c, n), 1) % c)
    blk_bf = jnp.where((_iota((n, n), 0) // c) == (_iota((n, n), 1) // c), 1.0, 0.0).astype(bf16)
    to_blk = lambda side: jnp.concatenate([side] * HEADS, axis=0) * blk_bf
    lbs = [low.astype(bf16) for low in lows]
    lblks = [to_blk(lb) for lb in lbs]
    base = jnp.where(xs < 2, 1.0, 0.0).astype(bf16)
    eye = jnp.where(xs == 0, 1.0, 0.0).astype(bf16)
    ts = [lb * base + eye for lb in lbs]
    for s in range(1, 6):
        join = jnp.where((xs >> s) == 1, 1.0, 0.0).astype(bf16)
        zs = [_dg(t, lblk, _NN).astype(bf16) for t, lblk in zip(ts, lblks)]
        ts = [t + _dg(z, to_blk(t), _NN).astype(bf16) * join for t, z in zip(ts, zs)]
    return ts


def _expand_heads(x, head_masks):
    return jnp.concatenate([jnp.where(m, x, 0.0) for m in head_masks], axis=0)


def _collapse_heads(x, c):
    return x[0:c] + x[c:2 * c] + x[2 * c:3 * c] + x[3 * c:4 * c]


def _head_masks():
    lane = _iota((1, HW), 1)
    return [(lane // HD) == h for h in range(HEADS)]


def _block_mask():
    return (_iota((HW, HW), 0) // HD) == (_iota((HW, HW), 1) // HD)


def _mod_body(c_ref, w_ref, b_ref, o_ref):
    cond = _silu(c_ref[...])
    o_ref[0] = _mm3(cond, w_ref[0]) + b_ref[0]


def _modulation(c, ada_w, ada_b):
    depth, _, n = ada_w.shape
    bsz = c.shape[0]
    nb = 512
    return pl.pallas_call(
        _mod_body,
        out_shape=jax.ShapeDtypeStruct((depth, bsz, n), f32),
        grid=(depth, n // nb),
        in_specs=[pl.BlockSpec((bsz, D_MODEL), lambda l, j: (0, 0)),
                  pl.BlockSpec((1, D_MODEL, nb), lambda l, j: (l, 0, j)),
                  pl.BlockSpec((1, 1, nb), lambda l, j: (l, 0, j))],
        out_specs=pl.BlockSpec((1, bsz, nb), lambda l, j: (l, 0, j)),
        compiler_params=pltpu.CompilerParams(dimension_semantics=("parallel", "parallel")),
        name="adaln_mod",
    )(c, ada_w, ada_b.reshape(depth, 1, n))


def _rope_body(cos_ref, sin_ref, *, tt):
    shape = (tt, HW // 2)
    pos = (pl.program_id(0) * tt + _iota(shape, 0)).astype(f32)
    lane = _iota(shape, 1)
    half = HD // 2
    inv_freq = jnp.exp((lane % half).astype(f32) * (-math.log(ROPE_BASE) / half))
    ang = pos * inv_freq
    co = jnp.cos(ang)
    s = jnp.sin(ang)
    si = jnp.where((lane % HD) < half, -s, s)
    cos_ref[...] = jnp.concatenate([co, co], axis=1)
    sin_ref[...] = jnp.concatenate([si, si], axis=1)


def _rope_tables(t):
    tt = min(t, 512)
    return pl.pallas_call(
        functools.partial(_rope_body, tt=tt),
        out_shape=(jax.ShapeDtypeStruct((t, HW), f32), jax.ShapeDtypeStruct((t, HW), f32)),
        grid=(t // tt,),
        out_specs=(pl.BlockSpec((tt, HW), lambda i: (i, 0)), pl.BlockSpec((tt, HW), lambda i: (i, 0))),
        compiler_params=pltpu.CompilerParams(dimension_semantics=("parallel",)),
        name="rope_tables",
    )()


def _proj_body(x_ref, mod_ref, nw_ref, w_ref, cw_ref, p_ref, carry_ref):
    tm = x_ref.shape[1]

    @pl.when(pl.program_id(1) == 0)
    def _():
        carry_ref[...] = jnp.zeros_like(carry_ref)

    x = x_ref[0]
    m = mod_ref[0]
    y = x * lax.rsqrt(jnp.mean(x * x, axis=-1, keepdims=True) + NORM_EPS) * nw_ref[...]
    h = (y * (1.0 + m[1:2]) + m[0:1]).astype(bf16)
    p_ref[0] = jnp.dot(h, w_ref[...], preferred_element_type=f32)

    for gi in range(3):
        cols = slice(gi * GDN_W, (gi + 1) * GDN_W)
        raw = p_ref[0, :, cols]
        xw = jnp.concatenate([carry_ref[:, cols], raw], axis=0)
        cw = cw_ref[:, cols]
        z = _silu(cw[3:4] * raw + cw[2:3] * xw[7:7 + tm] + cw[1:2] * xw[6:6 + tm] + cw[0:1] * xw[5:5 + tm])
        carry_ref[:, cols] = raw[tm - 8:tm]
        if gi < 2:
            for hd in range(HEADS):
                seg = z[:, hd * GDN_D:(hd + 1) * GDN_D]
                p_ref[0, :, gi * GDN_W + hd * GDN_D:gi * GDN_W + (hd + 1) * GDN_D] = seg * lax.rsqrt(
                    jnp.sum(seg * seg, axis=-1, keepdims=True) + L2_EPS)
        else:
            p_ref[0, :, cols] = z


def _projection(x, mod, norm_w, w_in_r, conv_w, tm):
    bsz, t, _ = x.shape
    return pl.pallas_call(
        _proj_body,
        out_shape=jax.ShapeDtypeStruct((bsz, t, P_COLS), f32),
        grid=(bsz, t // tm),
        in_specs=[pl.BlockSpec((1, tm, D_MODEL), lambda b, i: (b, i, 0)),
                  pl.BlockSpec((1, 6, D_MODEL), lambda b, i: (b, 0, 0)),
                  pl.BlockSpec((1, D_MODEL), lambda b, i: (0, 0)),
                  pl.BlockSpec((D_MODEL, P_COLS), lambda b, i: (0, 0), pipeline_mode=pl.Buffered(1)),
                  pl.BlockSpec((GDN_CONV, 3 * GDN_W), lambda b, i: (0, 0))],
        out_specs=pl.BlockSpec((1, tm, P_COLS), lambda b, i: (b, i, 0)),
        scratch_shapes=[pltpu.VMEM((8, 3 * GDN_W), f32)],
        compiler_params=pltpu.CompilerParams(dimension_semantics=("parallel", "arbitrary"),
                                             vmem_limit_bytes=VMEM_LIMIT),
        name="in_proj",
    )(x, mod, norm_w.reshape(1, D_MODEL), w_in_r, conv_w)


def _gdn_body(qkvz_ref, ba_ref, gp_ref, nw_ref, o_ref, s_ref, *, tt, nb):
    c = CHUNK
    nc = tt // c

    @pl.when(pl.program_id(1) == 0)
    def _():
        s_ref[...] = jnp.zeros_like(s_ref)

    gp = gp_ref[...]
    ii, jj = _iota((tt, tt), 0), _iota((tt, tt), 1)
    same = (ii // c) == (jj // c)
    same_f = jnp.where(same, 1.0, 0.0)
    cum_mat = jnp.where(same & (jj <= ii), 1.0, 0.0).astype(bf16)
    incl_side = _iota((c, tt), 0) >= (_iota((c, tt), 1) % c)
    strict_side = _iota((c, tt), 0) > (_iota((c, tt), 1) % c)
    sel = jnp.where(_iota((8, 128), 1) == _iota((8, 128), 0) + HEADS, 1.0, 0.0).astype(bf16)
    scale = GDN_D ** -0.5

    lows, attns, rhss, qdecs, kdecs, gls = [], [], [], [], [], []
    for bb in range(nb):
        qs, ks, vs = ([qkvz_ref[bb, :, gi * GDN_W + h * GDN_D:gi * GDN_W + (h + 1) * GDN_D] for h in range(HEADS)]
                      for gi in range(3))

        ba = ba_ref[bb]
        beta_all = _sigmoid(ba)
        g = -jnp.exp(gp[0:1]) * _softplus(ba + gp[1:2])
        gc_all = _mm_lexact(cum_mat, g)
        gl_all = jnp.concatenate(
            [jnp.broadcast_to(gc_all[(ic + 1) * c - 1:(ic + 1) * c], (c, 128)) for ic in range(nc)], axis=0)
        gcrows = _mm_lexact(sel, gc_all, _NT)
        gls.append(gl_all)
        for h in range(HEADS):
            q = qs[h] * scale
            k = ks[h]
            beta = beta_all[:, h:h + 1]
            gc = gc_all[:, HEADS + h:HEADS + h + 1]
            gl = gl_all[:, HEADS + h:HEADS + h + 1]
            gct = jnp.concatenate([jnp.broadcast_to(gc[ic * c:(ic + 1) * c], (c, c)) for ic in range(nc)], axis=1)
            dec = jnp.where(incl_side, jnp.exp(jnp.where(incl_side, gct - gcrows[h:h + 1, :], 0.0)), 0.0)
            kb = k * beta
            eg = jnp.exp(gc)
            gram = _dg(jnp.concatenate([kb.astype(bf16), q.astype(bf16)], axis=0), k.astype(bf16), _NT)
            lows.append(-(_collapse_heads(gram[:tt] * same_f, c) * jnp.where(strict_side, dec, 0.0)))
            attns.append((_collapse_heads(gram[tt:] * same_f, c) * dec).astype(bf16))
            rhss.append(jnp.concatenate([vs[h] * beta, kb * eg], axis=1))
            qdecs.append((q * eg).astype(bf16))
            kdecs.append((k * jnp.exp(gl - gc)).astype(bf16))
    same_bf = same_f.astype(bf16)
    tmats = [jnp.concatenate([ts] * nc, axis=0) * same_bf for ts in _tri_inv_side_many(lows, c)]
    sols = [_dg(tm, rhs.astype(bf16), _NN) for tm, rhs in zip(tmats, rhss)]

    chains = [(bb, h) for bb in range(nb) for h in range(HEADS)]
    rws = lambda ic: slice(ic * c, (ic + 1) * c)
    wqs = [[jnp.concatenate([sols[n][rws(ic), GDN_D:].astype(bf16), qdecs[n][rws(ic)]], axis=0) for ic in range(nc)]
           for n in range(len(chains))]
    cur = [s_ref[bb, h] for bb, h in chains]
    v_new = [[None] * nc for _ in chains]
    cross = [[None] * nc for _ in chains]
    for ic in range(nc):
        ws = [_dg(wqs[n][ic], cur[n].astype(bf16), _NN) for n in range(len(chains))]
        for n, (bb, h) in enumerate(chains):
            v_new[n][ic] = (sols[n][rws(ic), :GDN_D] - ws[n][:c]).astype(bf16)
            cross[n][ic] = ws[n][c:]
            d = jnp.exp(gls[bb][ic * c:ic * c + 1, HEADS + h:HEADS + h + 1])
            cur[n] = cur[n] * d + _dg(kdecs[n][rws(ic)], v_new[n][ic], _TN)
    for n, (bb, h) in enumerate(chains):
        s_ref[bb, h] = cur[n]

    nw = nw_ref[...]
    items = [(n, ic) for ic in range(nc) for n in range(len(chains))]
    outs = [cross[n][ic] + _dg(attns[n][:, ic * c:(ic + 1) * c], v_new[n][ic], _NN) for n, ic in items]
    for (n, ic), o in zip(items, outs):
        bb, h = chains[n]
        o = o * lax.rsqrt(jnp.mean(o * o, axis=-1, keepdims=True) + NORM_EPS) * nw
        z = qkvz_ref[bb, rws(ic), 3 * GDN_W + h * GDN_D:3 * GDN_W + (h + 1) * GDN_D]
        o_ref[bb, rws(ic), h * GDN_D:(h + 1) * GDN_D] = o * _silu(z)


def _gdn(p, a_log, dt_bias, norm_w, tt, nb):
    bsz, t, _ = p.shape
    gp = jnp.zeros((8, 128), f32)
    gp = gp.at[0, HEADS:2 * HEADS].set(a_log).at[1, HEADS:2 * HEADS].set(dt_bias)
    return pl.pallas_call(
        functools.partial(_gdn_body, tt=tt, nb=nb),
        out_shape=jax.ShapeDtypeStruct((bsz, t, GDN_W), f32),
        grid=(bsz // nb, t // tt),
        in_specs=[pl.BlockSpec((nb, tt, 4 * GDN_W), lambda b, i: (b, i, P_GDN // (4 * GDN_W))),
                  pl.BlockSpec((nb, tt, 128), lambda b, i: (b, i, P_BA // 128)),
                  pl.BlockSpec((8, 128), lambda b, i: (0, 0)),
                  pl.BlockSpec((1, GDN_D), lambda b, i: (0, 0))],
        out_specs=pl.BlockSpec((nb, tt, GDN_W), lambda b, i: (b, i, 0)),
        scratch_shapes=[pltpu.VMEM((nb, HEADS, GDN_D, GDN_D), f32)],
        compiler_params=pltpu.CompilerParams(dimension_semantics=("parallel", "arbitrary"),
                                             vmem_limit_bytes=VMEM_LIMIT),
        name="gdn_mixer",
    )(p, p, gp, norm_w.reshape(1, GDN_D))


def _seg_stat(x, blk_bf):
    return _mm_r16(x, blk_bf)


def _ret_body(p_ref, cos_ref, sin_ref, o_ref, r_ref, *, tt, nb):
    c = CHUNK
    nc = tt // c

    @pl.when(pl.program_id(1) == 0)
    def _():
        r_ref[...] = jnp.zeros_like(r_ref)

    lane_t = _iota((tt, HW), 1)
    first_half = (lane_t % HD) < (HD // 2)
    cosf = cos_ref[...]
    sins = sin_ref[...]

    def rotary(x):
        swapped = jnp.where(first_half, pltpu.roll(x, HW - HD // 2, 1), pltpu.roll(x, HD // 2, 1))
        return x * cosf + swapped * sins

    hm = _head_masks()
    blk = _block_mask()
    blk_bf = jnp.where(blk, 1.0, 0.0).astype(bf16)
    lg = jnp.zeros((1, HW), f32)
    for h in range(HEADS):
        lg = jnp.where(hm[h], math.log(1.0 - 2.0 ** (-5.0 - h)), lg)
    pos = _iota((c, HW), 0).astype(f32)
    q_decay = jnp.exp((pos + 1.0) * lg)
    k_decay = jnp.exp((c - 1.0 - pos) * lg)
    chunk_decay = jnp.exp(float(c) * lg)
    rel = (_iota((c, HW), 0) - (_iota((c, HW), 1) % c)).astype(f32)
    dmask = jnp.where(rel >= 0, jnp.exp(jnp.maximum(rel, 0.0) * lg), 0.0)

    items = [(bb, ic) for bb in range(nb) for ic in range(nc)]
    rws = lambda ic: slice(ic * c, (ic + 1) * c)
    qs = [rotary(p_ref[bb, :, 0:HW]) for bb in range(nb)]
    ks = [rotary(p_ref[bb, :, HW:2 * HW]) * (HD ** -0.5) for bb in range(nb)]
    kes = [_expand_heads(ks[bb][rws(ic)], hm).astype(bf16) for bb, ic in items]
    ves = [_expand_heads(p_ref[bb, rws(ic), 2 * HW:3 * HW], hm).astype(bf16) for bb, ic in items]
    scores = [(_dg(qs[bb][rws(ic)].astype(bf16), ke, _NT) * dmask).astype(bf16) for (bb, ic), ke in zip(items, kes)]
    inner = [_dg(s, ve, _NN) for s, ve in zip(scores, ves)]
    upds = [jnp.where(blk, _mm(ks[bb][rws(ic)] * k_decay, p_ref[bb, rws(ic), 2 * HW:3 * HW], _TN), 0.0)
            for bb, ic in items]

    states = []
    for bb in range(nb):
        cur = r_ref[bb]
        for ic in range(nc):
            states.append(cur.astype(bf16))
            cur = cur * chunk_decay + upds[bb * nc + ic]
        r_ref[bb] = cur
    cross = [_dg((qs[bb][rws(ic)] * q_decay).astype(bf16), st, _NN) for (bb, ic), st in zip(items, states)]

    for bb in range(nb):
        o = jnp.concatenate([inner[bb * nc + ic] + cross[bb * nc + ic] for ic in range(nc)], axis=0)
        mu = _seg_stat(o, blk_bf) * (1.0 / HD)
        oc = o - mu
        var = _seg_stat(oc * oc, blk_bf) * (1.0 / HD)
        o_ref[bb] = oc * lax.rsqrt(var + RET_GN_EPS) * _silu(p_ref[bb, :, 3 * HW:4 * HW])


def _retention(p, cos_t, sin_t, tt, nb):
    bsz, t, _ = p.shape
    return pl.pallas_call(
        functools.partial(_ret_body, tt=tt, nb=nb),
        out_shape=jax.ShapeDtypeStruct((bsz, t, HW), f32),
        grid=(bsz // nb, t // tt),
        in_specs=[pl.BlockSpec((nb, tt, 4 * HW), lambda b, i: (b, i, P_RET // (4 * HW))),
                  pl.BlockSpec((tt, HW), lambda b, i: (i, 0)),
                  pl.BlockSpec((tt, HW), lambda b, i: (i, 0))],
        out_specs=pl.BlockSpec((nb, tt, HW), lambda b, i: (b, i, 0)),
        scratch_shapes=[pltpu.VMEM((nb, HW, HW), f32)],
        compiler_params=pltpu.CompilerParams(dimension_semantics=("parallel", "arbitrary"),
                                             vmem_limit_bytes=VMEM_LIMIT),
        name="retention_mixer",
    )(p, cos_t, sin_t)


_RW_W0, _RW_A0, _RW_KK, _RW_KA, _RW_RK, _RW_LNW, _RW_LNB, _RW_V0 = range(8)


def _rwkv_body(*refs, tt, nb, first):
    if first:
        (p_ref, mu_ref, par_ref, w2_ref, a2_ref, g2_ref, o_ref, vf_out_ref, s_ref, carry_ref) = refs
    else:
        (p_ref, vf_ref, mu_ref, par_ref, w2_ref, a2_ref, g2_ref, v1_ref, v2_ref, o_ref, s_ref, carry_ref) = refs
    c = CHUNK
    nc = tt // c

    @pl.when(pl.program_id(1) == 0)
    def _():
        s_ref[...] = jnp.zeros_like(s_ref)
        carry_ref[...] = jnp.zeros_like(carry_ref)

    blk = _block_mask()
    blk_bf = jnp.where(blk, 1.0, 0.0).astype(bf16)
    par = par_ref[...]
    row = lambda i: par[i:i + 1]
    hm = _head_masks()
    ii, jj = _iota((tt, tt), 0), _iota((tt, tt), 1)
    cum_mat = jnp.where(((ii // c) == (jj // c)) & (jj <= ii), 1.0, 0.0).astype(bf16)
    incl_side = _iota((c, HW), 0) >= (_iota((c, HW), 1) % c)
    strict_side = _iota((c, HW), 0) > (_iota((c, HW), 1) % c)
    ln_w, ln_b = row(_RW_LNW), row(_RW_LNB)

    a_abs, aes, akvs, a_rbs, ylocs = [], [], [], [], []
    per_row = []
    for bb in range(nb):
        raw = p_ref[bb]
        xw = jnp.concatenate([carry_ref[bb], raw], axis=0)
        carry_ref[bb] = raw[tt - 8:tt]
        pm = raw + (xw[7:7 + tt] - raw) * mu_ref[...]
        r = pm[:, 0:HW]
        k = pm[:, HW:2 * HW]
        v = pm[:, 2 * HW:3 * HW]
        lora = pm[:, 3 * HW:3 * HW + RWKV_LORA_W]
        w_log = -_softplus(-(row(_RW_W0) + _mm(jnp.tanh(lora), w2_ref[...]))) - 0.5
        lw = -jnp.exp(w_log)
        a = _sigmoid(row(_RW_A0) + _mm(lora, a2_ref[...]))
        gate = _mm(_sigmoid(lora), g2_ref[...])
        if first:
            vf_out_ref[bb] = v
        else:
            mix = _sigmoid(row(_RW_V0) + _mm(_mm(v, v1_ref[...]), v2_ref[...]))
            v = v + (vf_ref[bb] - v) * mix
        kk = k * row(_RW_KK)
        kk = kk * lax.rsqrt(_seg_stat(kk * kk, blk_bf) + L2_EPS)
        k = k * (1.0 + (a - 1.0) * row(_RW_KA))
        a_ = -kk
        b_ = kk * a
        bonus = _seg_stat(r * k * row(_RW_RK), blk_bf) * v

        gcum = _mm_lexact(cum_mat, lw)
        glast = jnp.concatenate(
            [jnp.broadcast_to(gcum[(ic + 1) * c - 1:(ic + 1) * c], (c, HW)) for ic in range(nc)], axis=0)
        e_neg = jnp.exp(-gcum)
        rt = (r * jnp.exp(gcum)).astype(bf16)
        at = a_ * jnp.exp(gcum - lw)
        kt = k * e_neg
        bt = b_ * e_neg
        e_rem = jnp.exp(glast - gcum)
        kd = (k * e_rem).astype(bf16)
        bd = (b_ * e_rem).astype(bf16)
        per_row.append((rt, v, kd, bd, glast, bonus, gate))

        for ic in range(nc):
            rows = slice(ic * c, (ic + 1) * c)
            ae, ke, be, ve = (_expand_heads(x[rows], hm).astype(bf16) for x in (at, kt, bt, v))
            lhs = jnp.concatenate([at[rows].astype(bf16), rt[rows]], axis=0)
            sk = _dg(lhs, ke, _NT)
            sb = _dg(lhs, be, _NT)
            a_abs.append(jnp.where(strict_side, sb[:c], 0.0))
            a_rbs.append(jnp.where(incl_side, sb[c:], 0.0).astype(bf16))
            kv = _mm(jnp.concatenate([jnp.where(strict_side, sk[:c], 0.0), jnp.where(incl_side, sk[c:], 0.0)],
                                     axis=0), ve)
            aes.append(ae)
            akvs.append(_expand_heads(kv[:c], hm).astype(bf16))
            ylocs.append(kv[c:])
    tsides = _tri_inv_side_many(a_abs, c)
    wps = [_dg(ts, ae, _NN) for ts, ae in zip(tsides, aes)]
    ups = [_dg(ts, akv, _NN) for ts, akv in zip(tsides, akvs)]

    pcs, ncs = [], []
    for bb in range(nb):
        rt, v, kd, bd, glast, bonus, gate = per_row[bb]
        for ic in range(nc):
            rows = slice(ic * c, (ic + 1) * c)
            n = bb * nc + ic
            pcs.append(jnp.where(blk, _mm(wps[n], bd[rows], _TN), 0.0).astype(bf16))
            ncs.append(jnp.where(blk, _mm(jnp.concatenate([v[rows], ups[n]], axis=0),
                                          jnp.concatenate([kd[rows], bd[rows]], axis=0), _TN), 0.0))

    states = [None] * (nb * nc)
    cur = [s_ref[bb] for bb in range(nb)]
    for ic in range(nc):
        for bb in range(nb):
            n = bb * nc + ic
            states[n] = cur[bb].astype(bf16)
            e_last = jnp.exp(per_row[bb][4][ic * c:ic * c + 1, :])
            cur[bb] = cur[bb] * e_last + _dg(states[n], pcs[n], _NN) + ncs[n]
    for bb in range(nb):
        s_ref[bb] = cur[bb]

    rws = lambda n: slice((n % nc) * c, (n % nc + 1) * c)
    rts = [per_row[n // nc][0][rws(n)] for n in range(nb * nc)]
    wss = [_dg(jnp.concatenate([wps[n].astype(bf16), rts[n]], axis=0), states[n], _NT) for n in range(nb * nc)]
    us = [ups[n] + wss[n][:c] for n in range(nb * nc)]
    ys = [wss[n][c:] + ylocs[n] + _mm(a_rbs[n], _expand_heads(us[n], hm)) for n in range(nb * nc)]
    for bb in range(nb):
        rt, v, kd, bd, glast, bonus, gate = per_row[bb]
        y = jnp.concatenate(ys[bb * nc:(bb + 1) * nc], axis=0)
        mu = _seg_stat(y, blk_bf) * (1.0 / HD)
        yc = y - mu
        var = _seg_stat(yc * yc, blk_bf) * (1.0 / HD)
        yn = yc * lax.rsqrt(var + RWKV_GN_EPS) * ln_w + ln_b
        o_ref[bb] = (yn + bonus) * gate


def _pad_rows(w, start, total):
    return jnp.zeros((total, w.shape[1]), f32).at[start:start + w.shape[0]].set(w)


def _rwkv(p, v_first, mu, w0, w2, a0, a2, g2, k_k, k_a, r_k, ln_w, ln_b, v0, v1, v2, tt, nb):
    bsz, t, _ = p.shape
    first = v_first is None
    mu_pad = jnp.zeros((1, 4 * HW), f32).at[0, :mu.shape[0]].set(mu)
    par = jnp.stack([w0, a0, k_k, k_a, r_k.reshape(HW), ln_w, ln_b,
                     jnp.zeros((HW,), f32) if first else v0], axis=0)
    w2p = _pad_rows(w2, 0, RWKV_LORA_W).astype(bf16)
    a2p = _pad_rows(a2, 32, RWKV_LORA_W).astype(bf16)
    g2p = _pad_rows(g2, 64, RWKV_LORA_W).astype(bf16)
    full = lambda shape: pl.BlockSpec(shape, lambda b, i: (0,) * len(shape))
    tile = pl.BlockSpec((nb, tt, HW), lambda b, i: (b, i, 0))
    p_spec = pl.BlockSpec((nb, tt, 4 * HW), lambda b, i: (b, i, P_RWKV // (4 * HW)))
    common = [full((1, 4 * HW)), full((8, HW)), full((RWKV_LORA_W, HW)), full((RWKV_LORA_W, HW)),
              full((RWKV_LORA_W, HW))]
    if first:
        in_specs = [p_spec] + common
        args = (p, mu_pad, par, w2p, a2p, g2p)
        out_shape = (jax.ShapeDtypeStruct((bsz, t, HW), f32), jax.ShapeDtypeStruct((bsz, t, HW), f32))
        out_specs = (tile, tile)
    else:
        v1p = jnp.zeros((HW, 128), f32).at[:, :v1.shape[1]].set(v1).astype(bf16)
        v2p = _pad_rows(v2, 0, 128).astype(bf16)
        in_specs = [p_spec, tile] + common + [full((HW, 128)), full((128, HW))]
        args = (p, v_first, mu_pad, par, w2p, a2p, g2p, v1p, v2p)
        out_shape = jax.ShapeDtypeStruct((bsz, t, HW), f32)
        out_specs = tile
    res = pl.pallas_call(
        functools.partial(_rwkv_body, tt=tt, nb=nb, first=first),
        out_shape=out_shape,
        grid=(bsz // nb, t // tt),
        in_specs=in_specs,
        out_specs=out_specs,
        scratch_shapes=[pltpu.VMEM((nb, HW, HW), f32),
                        pltpu.VMEM((nb, 8, 4 * HW), f32)],
        compiler_params=pltpu.CompilerParams(dimension_semantics=("parallel", "arbitrary"),
                                             vmem_limit_bytes=VMEM_LIMIT),
        name="rwkv7_mixer_first" if first else "rwkv7_mixer",
    )(*args)
    return res if first else (res, v_first)


def _mlp_body(x_ref, oa_ref, ob_ref, oc_ref, mod_ref, nw_ref, wo_ref, wu_ref, wd_ref, fw_ref, o_ref, *, last):
    m = mod_ref[0]
    mix = (jnp.dot(oa_ref[0].astype(bf16), wo_ref[0:GDN_W, :], preferred_element_type=f32)
           + jnp.dot(ob_ref[0].astype(bf16), wo_ref[GDN_W:GDN_W + HW, :], preferred_element_type=f32)
           + jnp.dot(oc_ref[0].astype(bf16), wo_ref[GDN_W + HW:, :], preferred_element_type=f32))
    x = x_ref[0] + m[2:3] * mix
    y = x * lax.rsqrt(jnp.mean(x * x, axis=-1, keepdims=True) + NORM_EPS) * nw_ref[...]
    h = (y * (1.0 + m[4:5]) + m[3:4]).astype(bf16)
    acc = jnp.zeros(x.shape, f32)
    nb = 1024
    for j in range(D_FF // nb):
        hid = jnp.dot(h, wu_ref[:, j * nb:(j + 1) * nb], preferred_element_type=f32)
        hid = jnp.square(jnp.maximum(hid, 0.0))
        acc = acc + jnp.dot(hid.astype(bf16), wd_ref[j * nb:(j + 1) * nb, :], preferred_element_type=f32)
    x = x + m[5:6] * acc
    if last:
        x = x * lax.rsqrt(jnp.mean(x * x, axis=-1, keepdims=True) + NORM_EPS) * fw_ref[...]
    o_ref[0] = x


def _out_mlp(x, oa, ob, oc, mod, norm_w, w_out, w_up, w_down, final_w, last, tm):
    bsz, t, _ = x.shape
    tile = lambda w: pl.BlockSpec((1, tm, w), lambda b, i: (b, i, 0))
    const = lambda shape: pl.BlockSpec(shape, lambda b, i: (0, 0), pipeline_mode=pl.Buffered(1))
    return pl.pallas_call(
        functools.partial(_mlp_body, last=last),
        out_shape=jax.ShapeDtypeStruct((bsz, t, D_MODEL), f32),
        grid=(bsz, t // tm),
        in_specs=[tile(D_MODEL), tile(GDN_W), tile(HW), tile(HW),
                  pl.BlockSpec((1, 6, D_MODEL), lambda b, i: (b, 0, 0)),
                  pl.BlockSpec((1, D_MODEL), lambda b, i: (0, 0)),
                  const((D_MODEL, D_MODEL)), const((D_MODEL, D_FF)), const((D_FF, D_MODEL)),
                  pl.BlockSpec((1, D_MODEL), lambda b, i: (0, 0))],
        out_specs=tile(D_MODEL),
        compiler_params=pltpu.CompilerParams(dimension_semantics=("parallel", "parallel"),
                                             vmem_limit_bytes=VMEM_LIMIT),
        name="out_mlp_last" if last else "out_mlp",
    )(x, oa, ob, oc, mod, norm_w.reshape(1, D_MODEL), w_out, w_up, w_down, final_w.reshape(1, D_MODEL))


def _reorder_w_in(w):
    gdn_main = 4 * GDN_W
    ba = 2 * HEADS
    pad = jnp.zeros((w.shape[0], P_COLS - w.shape[1]), w.dtype)
    return jnp.concatenate([w[:, :gdn_main], w[:, gdn_main + ba:], w[:, gdn_main:gdn_main + ba], pad], axis=1)


def kernel(x, c, ada_w, ada_b, norm1_w, norm2_w, w_in, gdn_conv_w, gdn_a_log, gdn_dt_bias, gdn_norm_w, rwkv_mu, rwkv_w0, rwkv_w2, rwkv_a0, rwkv_a2, rwkv_g2, rwkv_k_k, rwkv_k_a, rwkv_r_k, rwkv_ln_w, rwkv_ln_b, rwkv_v0, rwkv_v1, rwkv_v2, w_out, w_up, w_down, final_norm_w):
    bsz, t, _ = x.shape
    depth = ada_w.shape[0]
    tm = min(t, 512)
    tt = min(t, 256)
    nb = 4 if bsz % 4 == 0 else (2 if bsz % 2 == 0 else 1)
    mod = _modulation(c, ada_w, ada_b).reshape(depth, bsz, 6, D_MODEL)
    cos_t, sin_t = _rope_tables(t)
    v_first = None
    for l in range(depth):
        p = _projection(x, mod[l], norm1_w[l], _reorder_w_in(w_in[l].astype(bf16)), gdn_conv_w[l], tm)
        o_a = _gdn(p, gdn_a_log[l], gdn_dt_bias[l], gdn_norm_w[l], tt, nb)
        o_b = _retention(p, cos_t, sin_t, tt, nb)
        o_c, v_first = _rwkv(p, v_first, rwkv_mu[l], rwkv_w0[l], rwkv_w2[l], rwkv_a0[l], rwkv_a2[l], rwkv_g2[l],
                             rwkv_k_k[l], rwkv_k_a[l], rwkv_r_k[l], rwkv_ln_w[l], rwkv_ln_b[l],
                             None if l == 0 else rwkv_v0[l - 1], None if l == 0 else rwkv_v1[l - 1],
                             None if l == 0 else rwkv_v2[l - 1], tt, nb)
        x = _out_mlp(x, o_a, o_b, o_c, mod[l], norm2_w[l], w_out[l].astype(bf16), w_up[l].astype(bf16),
                     w_down[l].astype(bf16), final_norm_w, l == depth - 1, tm)
    return x
```
